```python
import math
import jax, jax.numpy as jnp
from jax import lax
import numpy as np

D_MODEL = 1024
BATCH = 32
SEQ = 2048
DEPTH = 2
DEC_BATCH = 4
DEC_SEQ = 4096
PAST_LEN = 128

HEAD_DIM = 64
GRID_W = 64
Q_BLOCK = 128
EPS = 1e-6
NEG = -1e30

A_Q_HEADS = 8
A_KV_HEADS = 2
AXIAL_THETA = 10000.0
B_HEADS = 4
LAMBDA_SCALE = 0.1
C_HEADS = 16
C_PATTERNS = ((128, 1), (512, 4), (2048, 16))
ROPE_THETA = 500000.0
ROPE_DIMS = HEAD_DIM // 4

A_Q_W = A_Q_HEADS * HEAD_DIM
A_KV_W = A_KV_HEADS * HEAD_DIM
B_QK_W = B_HEADS * 2 * HEAD_DIM
B_V_W = B_HEADS * 2 * HEAD_DIM
EVEN_MIX_W = A_Q_W + B_V_W
EVEN_SPLITS = (A_Q_W, A_KV_W, A_KV_W, B_QK_W, B_QK_W, B_V_W, EVEN_MIX_W)
EVEN_IN_W = sum(EVEN_SPLITS)
C_W = C_HEADS * HEAD_DIM
ODD_SPLITS = (C_W, C_W, C_W, C_W)
ODD_IN_W = sum(ODD_SPLITS)

kernel_name = "hybrid_gqa_diff_dilated_encoder"


def _split_points(sizes):
    pts, acc = [], 0
    for s in sizes[:-1]:
        acc += s
        pts.append(acc)
    return pts


def rms_norm(x, g):
    x32 = x.astype(jnp.float32)
    y = x32 * lax.rsqrt(jnp.mean(x32 * x32, axis=-1, keepdims=True) + EPS)
    return (y * g.astype(jnp.float32)).astype(x.dtype)


def rotate_half(x):
    x1, x2 = jnp.split(x, 2, axis=-1)
    return jnp.concatenate([-x2, x1], axis=-1)


def rope_angles(pos, n_dims, theta):
    freqs = theta ** (-jnp.arange(0, n_dims, 2, dtype=jnp.float32) / n_dims)
    ang = pos[:, None] * freqs[None, :]
    ang = jnp.concatenate([ang, ang], axis=-1)
    return jnp.cos(ang), jnp.sin(ang)


def apply_rope(x, cos, sin):
    S, n = cos.shape
    shp = (1, S) + (1,) * (x.ndim - 3) + (n,)
    c = cos.reshape(shp).astype(x.dtype)
    s = sin.reshape(shp).astype(x.dtype)
    return x * c + rotate_half(x) * s


def partial_rope(x):
    S = x.shape[1]
    pos = jnp.arange(S, dtype=jnp.float32)
    cos, sin = rope_angles(pos, ROPE_DIMS, ROPE_THETA)
    return jnp.concatenate([apply_rope(x[..., :ROPE_DIMS], cos, sin), x[..., ROPE_DIMS:]], axis=-1)


def axial_rope(x):
    S = x.shape[1]
    rows = S // GRID_W
    row = jnp.broadcast_to(jnp.arange(rows, dtype=jnp.float32)[:, None], (rows, GRID_W)).reshape(-1)
    col = jnp.broadcast_to(jnp.arange(GRID_W, dtype=jnp.float32)[None, :], (rows, GRID_W)).reshape(-1)
    half = HEAD_DIM // 2
    xr = apply_rope(x[..., :half], *rope_angles(row, half, AXIAL_THETA))
    xc = apply_rope(x[..., half:], *rope_angles(col, half, AXIAL_THETA))
    return jnp.concatenate([xr, xc], axis=-1)


def split_q_blocks(q):
    B, S = q.shape[:2]
    return jnp.moveaxis(q.reshape((B, S // Q_BLOCK, Q_BLOCK) + q.shape[2:]), 1, 0)


def merge_q_blocks(o):
    nb, B = o.shape[:2]
    return jnp.moveaxis(o, 0, 1).reshape((B, nb * Q_BLOCK) + o.shape[3:])


def gqa_attention(q, k, v):
    B, S, Hq, D = q.shape
    Hkv = k.shape[2]
    G = Hq // Hkv
    scale = D ** -0.5
    qb = split_q_blocks(q.reshape(B, S, Hkv, G, D))

    def block(qi):
        s = jnp.einsum('bqhgd,bkhd->bhgqk', qi, k, preferred_element_type=jnp.float32) * scale
        p = jax.nn.softmax(s, axis=-1).astype(v.dtype)
        return jnp.einsum('bhgqk,bkhd->bqhgd', p, v)

    return merge_q_blocks(lax.map(block, qb)).reshape(B, S, Hq, D)


def diff_attention(q, k, v, lam):
    D = q.shape[-1]
    scale = D ** -0.5
    qb = split_q_blocks(q)

    def block(qi):
        s = jnp.einsum('bqchd,bkchd->bchqk', qi, k, preferred_element_type=jnp.float32) * scale
        p = jax.nn.softmax(s, axis=-1)
        a = (p[:, 0] - lam * p[:, 1]).astype(v.dtype)
        return jnp.einsum('bhqk,bkhe->bqhe', a, v)

    return merge_q_blocks(lax.map(block, qb))


def band_attention(q, k, v, radius):
    B, N, L, H, D = q.shape
    blk = radius
    nb = -(-L // blk)
    Lp = nb * blk
    scale = D ** -0.5
    qp = jnp.pad(q, ((0, 0), (0, 0), (0, Lp - L), (0, 0), (0, 0)))
    kv_pad = ((0, 0), (0, 0), (blk, Lp - L + blk), (0, 0), (0, 0))
    kb = jnp.pad(k, kv_pad).reshape(B, N, nb + 2, blk, H, D)
    vb = jnp.pad(v, kv_pad).reshape(B, N, nb + 2, blk, H, D)
    kw = jnp.concatenate([kb[:, :, :-2], kb[:, :, 1:-1], kb[:, :, 2:]], axis=3)
    vw = jnp.concatenate([vb[:, :, :-2], vb[:, :, 1:-1], vb[:, :, 2:]], axis=3)
    qb = qp.reshape(B, N, nb, blk, H, D)
    qpos = jnp.arange(nb)[:, None] * blk + jnp.arange(blk)[None, :]
    kpos = (jnp.arange(nb)[:, None] - 1) * blk + jnp.arange(3 * blk)[None, :]
    valid = ((jnp.abs(kpos[:, None, :] - qpos[:, :, None]) <= radius)
             & (kpos[:, None, :] >= 0) & (kpos[:, None, :] < L))
    s = jnp.einsum('bniqhd,bnikhd->bnihqk', qb, kw, preferred_element_type=jnp.float32) * scale
    s = jnp.where(valid[None, None, :, None], s, NEG)
    m = jnp.max(s, axis=-1, keepdims=True)
    e = jnp.exp(s - m)
    den = jnp.sum(e, axis=-1, keepdims=True)
    p = (e / den).astype(v.dtype)
    lse = (jnp.log(den) + m)[..., 0]
    o = jnp.einsum('bnihqk,bnikhd->bniqhd', p, vw).reshape(B, N, Lp, H, D)[:, :, :L]
    lse = jnp.swapaxes(lse, 3, 4).reshape(B, N, Lp, H)[:, :, :L]
    return o, lse


def dilated_branch(q, k, v, window, dilation):
    B, S, H, D = q.shape
    L = S // dilation
    radius = window // (2 * dilation)

    def strided(t):
        return jnp.swapaxes(t.reshape(B, L, dilation, H, D), 1, 2)

    o, lse = band_attention(strided(q), strided(k), strided(v), radius)
    o = jnp.swapaxes(o, 1, 2).reshape(B, S, H, D)
    lse = jnp.swapaxes(lse, 1, 2).reshape(B, S, H)
    return o, lse


def dilated_attention(q, k, v):
    outs, lses = [], []
    for window, dilation in C_PATTERNS:
        o, lse = dilated_branch(q, k, v, window, dilation)
        outs.append(o)
        lses.append(lse)
    w = jax.nn.softmax(jnp.stack(lses, axis=0), axis=0)
    o = jnp.sum(w[..., None] * jnp.stack(outs, axis=0).astype(jnp.float32), axis=0)
    return o.astype(q.dtype)


def even_layer(x, norm_g, w_in, w_out, a_q_norm, a_k_norm, b_q_norm, b_k_norm,
               lambda_q1, lambda_k1, lambda_q2, lambda_k2, b_subln, layer_idx):
    B, S, _ = x.shape
    h = rms_norm(x, norm_g)
    z = h @ w_in
    qa, ka, va, qb, kb, vb, gate = jnp.split(z, _split_points(EVEN_SPLITS), axis=-1)
    qa = axial_rope(rms_norm(qa.reshape(B, S, A_Q_HEADS, HEAD_DIM), a_q_norm))
    ka = axial_rope(rms_norm(ka.reshape(B, S, A_KV_HEADS, HEAD_DIM), a_k_norm))
    va = va.reshape(B, S, A_KV_HEADS, HEAD_DIM)
    oa = gqa_attention(qa, ka, va).reshape(B, S, A_Q_W)
    qb = jnp.swapaxes(qb.reshape(B, S, B_HEADS, 2, HEAD_DIM), 2, 3)
    kb = jnp.swapaxes(kb.reshape(B, S, B_HEADS, 2, HEAD_DIM), 2, 3)
    qb = partial_rope(rms_norm(qb, b_q_norm))
    kb = partial_rope(rms_norm(kb, b_k_norm))
    vb = vb.reshape(B, S, B_HEADS, 2 * HEAD_DIM)
    lambda_init = 0.8 - 0.6 * math.exp(-0.3 * layer_idx)
    f32 = jnp.float32
    lam = (jnp.exp(jnp.sum(lambda_q1.astype(f32) * lambda_k1.astype(f32)))
           - jnp.exp(jnp.sum(lambda_q2.astype(f32) * lambda_k2.astype(f32))) + lambda_init)
    ob = diff_attention(qb, kb, vb, lam)
    ob = (rms_norm(ob, b_subln) * (1.0 - lambda_init)).reshape(B, S, B_V_W)
    mix = jnp.concatenate([oa, ob], axis=-1) * jax.nn.silu(gate)
    return x + mix @ w_out


def odd_layer(x, norm_g, w_in, w_out, c_q_norm, c_k_norm):
    B, S, _ = x.shape
    h = rms_norm(x, norm_g)
    z = h @ w_in
    q, k, v, gate = jnp.split(z, _split_points(ODD_SPLITS), axis=-1)
    q = partial_rope(rms_norm(q.reshape(B, S, C_HEADS, HEAD_DIM), c_q_norm))
    k = partial_rope(rms_norm(k.reshape(B, S, C_HEADS, HEAD_DIM), c_k_norm))
    v = v.reshape(B, S, C_HEADS, HEAD_DIM)
    o = dilated_attention(q, k, v).reshape(B, S, C_W)
    return x + (o * jax.nn.silu(gate)) @ w_out


def trunk(x, even_params, odd_params):
    for l in range(DEPTH):
        if l % 2 == 0:
            x = even_layer(x, *even_params, layer_idx=l)
        else:
            x = odd_layer(x, *odd_params)
    return x


def setup_inputs(seed: int = 0) -> dict:
    key = jax.random.key(seed)
    ks = jax.random.split(key, 20)
    f32 = jnp.float32

    def nrm(k, shape, scale):
        return jax.random.normal(k, shape, f32) * scale

    def gain(k, n):
        return jnp.ones((n,), f32) + 0.02 * jax.random.normal(k, (n,), f32)

    return {
        "x_prompt": nrm(ks[0], (BATCH, SEQ, D_MODEL), 1.0),
        "x_sample": nrm(ks[1], (DEC_BATCH, DEC_SEQ, D_MODEL), 1.0),
        "norm0": gain(ks[2], D_MODEL),
        "w_in0": nrm(ks[3], (D_MODEL, EVEN_IN_W), D_MODEL ** -0.5),
        "w_out0": nrm(ks[4], (EVEN_MIX_W, D_MODEL), EVEN_MIX_W ** -0.5),
        "a_q_norm": gain(ks[5], HEAD_DIM),
        "a_k_norm": gain(ks[6], HEAD_DIM),
        "b_q_norm": gain(ks[7], HEAD_DIM),
        "b_k_norm": gain(ks[8], HEAD_DIM),
        "lambda_q1": nrm(ks[9], (HEAD_DIM,), LAMBDA_SCALE),
        "lambda_k1": nrm(ks[10], (HEAD_DIM,), LAMBDA_SCALE),
        "lambda_q2": nrm(ks[11], (HEAD_DIM,), LAMBDA_SCALE),
        "lambda_k2": nrm(ks[12], (HEAD_DIM,), LAMBDA_SCALE),
        "b_subln": gain(ks[13], 2 * HEAD_DIM),
        "norm1": gain(ks[14], D_MODEL),
        "w_in1": nrm(ks[15], (D_MODEL, ODD_IN_W), D_MODEL ** -0.5),
        "w_out1": nrm(ks[16], (C_W, D_MODEL), C_W ** -0.5),
        "c_q_norm": gain(ks[17], HEAD_DIM),
        "c_k_norm": gain(ks[18], HEAD_DIM),
    }


def reference(x_prompt, x_sample, norm0, w_in0, w_out0, a_q_norm, a_k_norm, b_q_norm, b_k_norm,
              lambda_q1, lambda_k1, lambda_q2, lambda_k2, b_subln, norm1, w_in1, w_out1,
              c_q_norm, c_k_norm):
    even_params = (norm0, w_in0, w_out0, a_q_norm, a_k_norm, b_q_norm, b_k_norm,
                   lambda_q1, lambda_k1, lambda_q2, lambda_k2, b_subln)
    odd_params = (norm1, w_in1, w_out1, c_q_norm, c_k_norm)
    y_prompt = trunk(x_prompt, even_params, odd_params)
    y_sample = trunk(x_sample, even_params, odd_params)
    return (y_prompt, y_sample)
```

```python
import functools
import math

import jax
import jax.numpy as jnp
from jax import lax
from jax.experimental import pallas as pl
from jax.experimental.pallas import tpu as pltpu

D_MODEL = 1024
HEAD_DIM = 64
GRID_W = 64
EPS = 1e-6
NEG = -1e30

A_Q_HEADS = 8
A_KV_HEADS = 2
AXIAL_THETA = 10000.0
B_HEADS = 4
C_HEADS = 16
C_PATTERNS = ((128, 1), (512, 4), (2048, 16))
ROPE_THETA = 500000.0
ROPE_DIMS = HEAD_DIM // 4

A_Q_W = A_Q_HEADS * HEAD_DIM
A_KV_W = A_KV_HEADS * HEAD_DIM
B_QK_W = B_HEADS * 2 * HEAD_DIM
B_V_W = B_HEADS * 2 * HEAD_DIM
EVEN_MIX_W = A_Q_W + B_V_W
C_W = C_HEADS * HEAD_DIM

LANES = 128
VMEM_LIMIT = 56 * 1024 * 1024
PROJ_ROWS = 512
ATTN_ROWS = 256
BAND_ROWS = 128
BAND_RADIUS = 64
QK_SCALE = HEAD_DIM ** -0.5

BF16 = jnp.bfloat16
F32 = jnp.float32


def _rope_tables(n_pos):
    pos = jnp.arange(n_pos, dtype=F32)
    lane = jnp.arange(HEAD_DIM)

    def angles(p, n_dims, theta):
        freqs = theta ** (-jnp.arange(0, n_dims, 2, dtype=F32) / n_dims)
        ang = p[:, None] * freqs[None, :]
        return jnp.concatenate([ang, ang], axis=-1)

    half = HEAD_DIM // 2
    row = jnp.floor(pos / GRID_W)
    col = pos - row * GRID_W
    ang = jnp.concatenate([angles(row, half, AXIAL_THETA), angles(col, half, AXIAL_THETA)], axis=-1)
    first = (lane % half) < (half // 2)
    ax_c = jnp.cos(ang)
    ax_sa = jnp.where(first[None, :], -jnp.sin(ang), 0.0)
    ax_sb = jnp.where(first[None, :], 0.0, jnp.sin(ang))

    angp = angles(pos, ROPE_DIMS, ROPE_THETA)
    pad = jnp.zeros((n_pos, HEAD_DIM - ROPE_DIMS), F32)
    cosp = jnp.concatenate([jnp.cos(angp), pad + 1.0], axis=-1)
    sinp = jnp.concatenate([jnp.sin(angp), pad], axis=-1)
    p_first = lane < (ROPE_DIMS // 2)
    p_second = (lane >= ROPE_DIMS // 2) & (lane < ROPE_DIMS)
    pr_c = cosp
    pr_sa = jnp.where(p_first[None, :], -sinp, 0.0)
    pr_sb = jnp.where(p_second[None, :], sinp, 0.0)

    two = lambda t: jnp.concatenate([t, t], axis=-1).astype(F32)
    return tuple(two(t) for t in (ax_c, ax_sa, ax_sb, pr_c, pr_sa, pr_sb))


def _head_mean_matrix():
    i = jnp.arange(LANES)
    same = (i[:, None] // HEAD_DIM) == (i[None, :] // HEAD_DIM)
    return jnp.where(same, 1.0 / HEAD_DIM, 0.0).astype(BF16)


def _rms_rows(x, g):
    return x * lax.rsqrt(jnp.mean(x * x, axis=-1, keepdims=True) + EPS) * g


def _head_norm_rope(x, gmat, gain, c, sa, sb, half, scale):
    ss = jnp.dot((x * x).astype(BF16), gmat, preferred_element_type=F32)
    y = x * lax.rsqrt(ss + EPS) * gain
    out = y * c + pltpu.roll(y, LANES - half, 1) * sa + pltpu.roll(y, half, 1) * sb
    if scale != 1.0:
        out = out * scale
    return out.astype(BF16)


def _proj0_kernel(x_ref, g_ref, w_ref, gmat_ref, aq_ref, ak_ref, bq_ref, bk_ref,
                  axc_ref, axsa_ref, axsb_ref, prc_ref, prsa_ref, prsb_ref,
                  qa_ref, ka_ref, va_ref, qb_ref, kb_ref, vb_ref, sg_ref):
    h = _rms_rows(x_ref[...], g_ref[...]).astype(BF16)
    gmat = gmat_ref[...]
    ax = (axc_ref[...], axsa_ref[...], axsb_ref[...])
    pr = (prc_ref[...], prsa_ref[...], prsb_ref[...])

    def proj(c0, width):
        return jnp.dot(h, w_ref[:, c0:c0 + width], preferred_element_type=F32)

    def normed(z, out_ref, gain, tabs, half, scale):
        for c in range(z.shape[1] // LANES):
            sl = slice(c * LANES, (c + 1) * LANES)
            out_ref[:, sl] = _head_norm_rope(z[:, sl], gmat, gain, *tabs, half, scale)

    c0 = 0
    normed(proj(c0, A_Q_W), qa_ref, aq_ref[...], ax, HEAD_DIM // 4, QK_SCALE)
    c0 += A_Q_W
    zkv = proj(c0, 2 * A_KV_W)
    normed(zkv[:, :A_KV_W], ka_ref, ak_ref[...], ax, HEAD_DIM // 4, 1.0)
    va_ref[...] = zkv[:, A_KV_W:].astype(BF16)
    c0 += 2 * A_KV_W
    normed(proj(c0, B_QK_W), qb_ref, bq_ref[...], pr, ROPE_DIMS // 2, QK_SCALE)
    c0 += B_QK_W
    normed(proj(c0, B_QK_W), kb_ref, bk_ref[...], pr, ROPE_DIMS // 2, 1.0)
    c0 += B_QK_W
    vb_ref[...] = proj(c0, B_V_W).astype(BF16)
    c0 += B_V_W
    gate = proj(c0, EVEN_MIX_W)
    sg_ref[...] = (gate * jax.nn.sigmoid(gate)).astype(BF16)


def _row_spec(width, rows=PROJ_ROWS):
    return pl.BlockSpec((rows, width), lambda i: (i, 0))


def _const_spec(shape):
    return pl.BlockSpec(shape, lambda i: (0,) * len(shape))


def _proj0(x2d, seq, norm_g, w_bf16, gmat, gains, tables):
    n = x2d.shape[0]
    pos_blocks = seq // PROJ_ROWS
    tab_spec = pl.BlockSpec((PROJ_ROWS, LANES), lambda i: (i % pos_blocks, 0))
    widths = (A_Q_W, A_KV_W, A_KV_W, B_QK_W, B_QK_W, B_V_W, EVEN_MIX_W)
    return pl.pallas_call(
        _proj0_kernel,
        grid=(n // PROJ_ROWS,),
        in_specs=[_row_spec(D_MODEL), _const_spec((1, D_MODEL)), _const_spec(w_bf16.shape),
                  _const_spec((LANES, LANES))] + [_const_spec((1, LANES))] * 4 + [tab_spec] * 6,
        out_specs=[_row_spec(w) for w in widths],
        out_shape=[jax.ShapeDtypeStruct((n, w), BF16) for w in widths],
        compiler_params=pltpu.CompilerParams(dimension_semantics=("parallel",),
                                             vmem_limit_bytes=VMEM_LIMIT),
        name="proj0",
    )(x2d, norm_g, w_bf16, gmat, *gains, *tables)


def _softmax_parts(q, k):
    s = lax.dot_general(q, k, (((1,), (1,)), ((), ())), preferred_element_type=F32)
    m = jnp.max(s, axis=-1, keepdims=True)
    e = jnp.exp(s - m)
    return e, jnp.sum(e, axis=-1, keepdims=True)


def _attn0_kernel(lam_init, lq1_ref, lk1_ref, lq2_ref, lk2_ref, subln_ref,
                  qa_ref, qb_ref, sg_ref, x_ref, ka_ref, va_ref, kb_ref, vb_ref, w_ref, y_ref):
    lam = (jnp.exp(jnp.sum(lq1_ref[...] * lk1_ref[...], axis=-1, keepdims=True))
           - jnp.exp(jnp.sum(lq2_ref[...] * lk2_ref[...], axis=-1, keepdims=True)) + lam_init)

    outs = []
    group = A_Q_HEADS // A_KV_HEADS
    for g in range(A_KV_HEADS):
        k = ka_ref[:, g * HEAD_DIM:(g + 1) * HEAD_DIM]
        v = va_ref[:, g * HEAD_DIM:(g + 1) * HEAD_DIM]
        for j in range(group):
            hq = g * group + j
            e, l = _softmax_parts(qa_ref[:, hq * HEAD_DIM:(hq + 1) * HEAD_DIM], k)
            o = jnp.dot(e.astype(BF16), v, preferred_element_type=F32)
            outs.append(o * (1.0 / l))

    subln = subln_ref[...]
    for hb in range(B_HEADS):
        c0 = hb * 2 * HEAD_DIM
        e0, l0 = _softmax_parts(qb_ref[:, c0:c0 + HEAD_DIM], kb_ref[:, c0:c0 + HEAD_DIM])
        e1, l1 = _softmax_parts(qb_ref[:, c0 + HEAD_DIM:c0 + 2 * HEAD_DIM],
                                kb_ref[:, c0 + HEAD_DIM:c0 + 2 * HEAD_DIM])
        a = (e0 * (1.0 / l0) - e1 * (lam / l1)).astype(BF16)
        o = jnp.dot(a, vb_ref[:, c0:c0 + 2 * HEAD_DIM], preferred_element_type=F32)
        outs.append(_rms_rows(o, subln) * (1.0 - lam_init))

    mix = (jnp.concatenate(outs, axis=-1) * sg_ref[...].astype(F32)).astype(BF16)
    y_ref[...] = x_ref[...] + jnp.dot(mix, w_ref[...], preferred_element_type=F32)


def _attn0(x3d, qa, ka, va, qb, kb, vb, sg, w_out, lams, subln, lam_init):
    b, s, _ = x3d.shape
    r3 = lambda t: t.reshape(b, s, t.shape[-1])
    qa, ka, va, qb, kb, vb, sg = map(r3, (qa, ka, va, qb, kb, vb, sg))
    row = lambda w: pl.BlockSpec((None, ATTN_ROWS, w), lambda bi, i: (bi, i, 0))
    full = lambda w: pl.BlockSpec((None, s, w), lambda bi, i: (bi, 0, 0))
    const = lambda shape: pl.BlockSpec(shape, lambda bi, i: (0,) * len(shape))
    return pl.pallas_call(
        functools.partial(_attn0_kernel, lam_init),
        grid=(b, s // ATTN_ROWS),
        in_specs=[const((1, HEAD_DIM))] * 4 + [const((1, 2 * HEAD_DIM)),
                  row(A_Q_W), row(B_QK_W), row(EVEN_MIX_W), row(D_MODEL),
                  full(A_KV_W), full(A_KV_W), full(B_QK_W), full(B_V_W), const(w_out.shape)],
        out_specs=row(D_MODEL),
        out_shape=jax.ShapeDtypeStruct((b, s, D_MODEL), F32),
        compiler_params=pltpu.CompilerParams(dimension_semantics=("parallel", "arbitrary"),
                                             vmem_limit_bytes=VMEM_LIMIT),
        name="attn0",
    )(*lams, subln, qa, qb, sg, x3d, ka, va, kb, vb, w_out)


def _proj1_kernel(x_ref, g_ref, w_ref, gmat_ref, cq_ref, ck_ref, prc_ref, prsa_ref, prsb_ref,
                  q_ref, k_ref, v_ref, sg_ref):
    h = _rms_rows(x_ref[...], g_ref[...]).astype(BF16)
    gmat = gmat_ref[...]
    pr = (prc_ref[...], prsa_ref[...], prsb_ref[...])

    def proj(c0, width):
        return jnp.dot(h, w_ref[:, c0:c0 + width], preferred_element_type=F32)

    for out_ref, gain_ref, c0, scale in ((q_ref, cq_ref, 0, QK_SCALE), (k_ref, ck_ref, C_W, 1.0)):
        gain = gain_ref[...]
        for half_w in range(2):
            z = proj(c0 + half_w * (C_W // 2), C_W // 2)
            for c in range(C_W // 2 // LANES):
                sl = slice(c * LANES, (c + 1) * LANES)
                out_ref[:, half_w * (C_W // 2) + c * LANES: half_w * (C_W // 2) + (c + 1) * LANES] = (
                    _head_norm_rope(z[:, sl], gmat, gain, *pr, ROPE_DIMS // 2, scale))
    v_ref[...] = proj(2 * C_W, C_W).astype(BF16)
    gate = proj(3 * C_W, C_W)
    sg_ref[...] = (gate * jax.nn.sigmoid(gate)).astype(BF16)


def _proj1(x2d, seq, norm_g, w_bf16, gmat, gains, tables):
    n = x2d.shape[0]
    pos_blocks = seq // PROJ_ROWS
    tab_spec = pl.BlockSpec((PROJ_ROWS, LANES), lambda i: (i % pos_blocks, 0))
    return pl.pallas_call(
        _proj1_kernel,
        grid=(n // PROJ_ROWS,),
        in_specs=[_row_spec(D_MODEL), _const_spec((1, D_MODEL)), _const_spec(w_bf16.shape),
                  _const_spec((LANES, LANES))] + [_const_spec((1, LANES))] * 2 + [tab_spec] * 3,
        out_specs=[_row_spec(C_W)] * 4,
        out_shape=[jax.ShapeDtypeStruct((n, C_W), BF16)] * 4,
        compiler_params=pltpu.CompilerParams(dimension_semantics=("parallel",),
                                             vmem_limit_bytes=VMEM_LIMIT),
        name="proj1",
    )(x2d, norm_g, w_bf16, gmat, *gains, *tables)


def _dilated_kernel(seq, q_ref, k_ref, v_ref, o_ref, qf, kf, vf, qd, kd, vd, acc, mx, den):
    qf[...] = q_ref[...].astype(F32)
    kf[...] = k_ref[...].astype(F32)
    vf[...] = v_ref[...].astype(F32)

    lane = lax.broadcasted_iota(jnp.int32, (1, LANES), 1)
    head_masks = [(lane < HEAD_DIM), (lane >= HEAD_DIM)]
    rr = lax.broadcasted_iota(jnp.int32, (BAND_ROWS, 2 * BAND_ROWS), 0)
    cc = lax.broadcasted_iota(jnp.int32, (BAND_ROWS, 2 * BAND_ROWS), 1)
    band = (cc >= rr) & (cc <= rr + 2 * BAND_RADIUS)
    zeros_pad = jnp.zeros((BAND_RADIUS, LANES), BF16)

    for p, (_, dil) in enumerate(C_PATTERNS):
        length = seq // dil
        padded = length + 2 * BAND_RADIUS
        blocks = length // BAND_ROWS
        for r in range(dil):
            rows = pl.ds(r, length, stride=dil) if dil > 1 else pl.ds(0, length)
            qd[r * length:(r + 1) * length, :] = qf[rows, :].astype(BF16)
            kd[r * padded:r * padded + BAND_RADIUS, :] = zeros_pad
            kd[r * padded + BAND_RADIUS:r * padded + BAND_RADIUS + length, :] = kf[rows, :].astype(BF16)
            kd[r * padded + BAND_RADIUS + length:(r + 1) * padded, :] = zeros_pad
            vd[r * padded:r * padded + BAND_RADIUS, :] = zeros_pad
            vd[r * padded + BAND_RADIUS:r * padded + BAND_RADIUS + length, :] = vf[rows, :].astype(BF16)
            vd[r * padded + BAND_RADIUS + length:(r + 1) * padded, :] = zeros_pad

        def block(t, carry, dil=dil, length=length, padded=padded, blocks=blocks, p=p):
            r = t // blocks
            i = t - r * blocks
            q0 = pl.multiple_of(r * length + i * BAND_ROWS, BAND_ROWS)
            k0 = pl.multiple_of(r * padded + i * BAND_ROWS, BAND_RADIUS)
            qb = qd[pl.ds(q0, BAND_ROWS), :]
            kb = kd[pl.ds(k0, 2 * BAND_ROWS), :]
            vb = vd[pl.ds(k0, 2 * BAND_ROWS), :]
            kpos = cc + (i * BAND_ROWS - BAND_RADIUS)
            valid = band & (kpos >= 0) & (kpos < length)
            o_parts, m_parts, l_parts = [], [], []
            for hm in head_masks:
                qh = jnp.where(hm, qb, jnp.zeros_like(qb))
                s = lax.dot_general(qh, kb, (((1,), (1,)), ((), ())), preferred_element_type=F32)
                s = jnp.where(valid, s, NEG)
                m = jnp.max(s, axis=-1, keepdims=True)
                e = jnp.exp(s - m)
                l = jnp.sum(e, axis=-1, keepdims=True)
                o_parts.append(jnp.dot(e.astype(BF16), vb, preferred_element_type=F32))
                m_parts.append(m)
                l_parts.append(l)
            o = jnp.where(head_masks[0], o_parts[0], o_parts[1])
            m2 = jnp.where(head_masks[0], m_parts[0], m_parts[1])
            l2 = jnp.where(head_masks[0], l_parts[0], l_parts[1])
            if dil > 1:
                dst = pl.ds(i * (BAND_ROWS * dil) + r, BAND_ROWS, stride=dil)
            else:
                dst = pl.ds(q0, BAND_ROWS)
            acc[p, dst, :] = o
            mx[p, dst, :] = m2
            den[p, dst, :] = l2
            return carry

        lax.fori_loop(0, dil * blocks, block, 0)

    m_all = jnp.maximum(jnp.maximum(mx[0], mx[1]), mx[2])
    num = jnp.zeros((seq, LANES), F32)
    dsum = jnp.zeros((seq, LANES), F32)
    for p in range(len(C_PATTERNS)):
        w = jnp.exp(mx[p] - m_all)
        num = num + w * acc[p]
        dsum = dsum + w * den[p]
    o_ref[...] = (num / dsum).astype(BF16)


def _dilated(q, k, v, b, s):
    r3 = lambda t: t.reshape(b, s, C_W)
    spec = pl.BlockSpec((None, s, LANES), lambda bi, hp: (bi, 0, hp))
    n_pat = len(C_PATTERNS)
    max_dil = max(d for _, d in C_PATTERNS)
    padded_rows = s + 2 * BAND_RADIUS * max_dil
    return pl.pallas_call(
        functools.partial(_dilated_kernel, s),
        grid=(b, C_W // LANES),
        in_specs=[spec] * 3,
        out_specs=spec,
        out_shape=jax.ShapeDtypeStruct((b, s, C_W), BF16),
        scratch_shapes=[pltpu.VMEM((s, LANES), F32)] * 3
        + [pltpu.VMEM((s, LANES), BF16)]
        + [pltpu.VMEM((padded_rows, LANES), BF16)] * 2
        + [pltpu.VMEM((n_pat, s, LANES), F32)] * 3,
        compiler_params=pltpu.CompilerParams(dimension_semantics=("parallel", "arbitrary"),
                                             vmem_limit_bytes=VMEM_LIMIT),
        name="dilated",
    )(r3(q), r3(k), r3(v))


def _out1_kernel(o_ref, sg_ref, x_ref, w_ref, y_ref):
    mix = (o_ref[...].astype(F32) * sg_ref[...].astype(F32)).astype(BF16)
    y_ref[...] = x_ref[...] + jnp.dot(mix, w_ref[...], preferred_element_type=F32)


def _out1(o2d, sg, x2d, w_bf16):
    n = x2d.shape[0]
    return pl.pallas_call(
        _out1_kernel,
        grid=(n // PROJ_ROWS,),
        in_specs=[_row_spec(C_W), _row_spec(C_W), _row_spec(D_MODEL), _const_spec(w_bf16.shape)],
        out_specs=_row_spec(D_MODEL),
        out_shape=jax.ShapeDtypeStruct((n, D_MODEL), F32),
        compiler_params=pltpu.CompilerParams(dimension_semantics=("parallel",),
                                             vmem_limit_bytes=VMEM_LIMIT),
        name="out1",
    )(o2d, sg, x2d, w_bf16)


def _trunk(x, p):
    b, s, _ = x.shape
    x2d = x.reshape(b * s, D_MODEL)
    qa, ka, va, qb, kb, vb, sg = _proj0(x2d, s, p["norm0"], p["w_in0"], p["gmat"], p["gains0"], p["tables"])
    y0 = _attn0(x, qa, ka, va, qb, kb, vb, sg, p["w_out0"], p["lams"], p["subln"], p["lam_init"])
    y0_2d = y0.reshape(b * s, D_MODEL)
    q, k, v, sg1 = _proj1(y0_2d, s, p["norm1"], p["w_in1"], p["gmat"], p["gains1"], p["tables"][3:])
    o = _dilated(q, k, v, b, s)
    y1 = _out1(o.reshape(b * s, C_W), sg1, y0_2d, p["w_out1"])
    return y1.reshape(b, s, D_MODEL)


def kernel(x_prompt, x_sample, norm0, w_in0, w_out0, a_q_norm, a_k_norm, b_q_norm, b_k_norm, lambda_q1, lambda_k1, lambda_q2, lambda_k2, b_subln, norm1, w_in1, w_out1, c_q_norm, c_k_norm):
    two = lambda g: jnp.concatenate([g, g]).reshape(1, LANES).astype(F32)
    max_seq = max(x_prompt.shape[1], x_sample.shape[1])
    params = {
        "norm0": norm0.reshape(1, D_MODEL), "norm1": norm1.reshape(1, D_MODEL),
        "w_in0": w_in0.astype(BF16), "w_out0": w_out0.astype(BF16),
        "w_in1": w_in1.astype(BF16), "w_out1": w_out1.astype(BF16),
        "gmat": _head_mean_matrix(),
        "gains0": tuple(two(g) for g in (a_q_norm, a_k_norm, b_q_norm, b_k_norm)),
        "gains1": tuple(two(g) for g in (c_q_norm, c_k_norm)),
        "tables": _rope_tables(max_seq),
        "lams": tuple(t.reshape(1, HEAD_DIM) for t in (lambda_q1, lambda_k1, lambda_q2, lambda_k2)),
        "subln": b_subln.reshape(1, 2 * HEAD_DIM),
        "lam_init": 0.8 - 0.6 * math.exp(-0.3 * 0),
    }
    return (_trunk(x_prompt, params), _trunk(x_sample, params))
```

```python
import functools
import math

import jax
import jax.numpy as jnp
from jax import lax
from jax.experimental import pallas as pl
from jax.experimental.pallas import tpu as pltpu

D_MODEL = 1024
HEAD_DIM = 64
GRID_W = 64
EPS = 1e-6
NEG = -1e30

A_Q_HEADS = 8
A_KV_HEADS = 2
AXIAL_THETA = 10000.0
B_HEADS = 4
C_HEADS = 16
C_PATTERNS = ((128, 1), (512, 4), (2048, 16))
ROPE_THETA = 500000.0
ROPE_DIMS = HEAD_DIM // 4

A_Q_W = A_Q_HEADS * HEAD_DIM
A_KV_W = A_KV_HEADS * HEAD_DIM
B_QK_W = B_HEADS * 2 * HEAD_DIM
B_V_W = B_HEADS * 2 * HEAD_DIM
EVEN_MIX_W = A_Q_W + B_V_W
C_W = C_HEADS * HEAD_DIM

LANES = 128
VMEM_LIMIT = 56 * 1024 * 1024
PROJ_ROWS = 512
ATTN_ROWS = 256
BAND_ROWS = 128
BAND_RADIUS = 64
BLOCK_UNROLL = 4
QK_SCALE = HEAD_DIM ** -0.5

BF16 = jnp.bfloat16
F32 = jnp.float32


def _rope_tables(n_pos):
    pos = jnp.arange(n_pos, dtype=F32)
    lane = jnp.arange(HEAD_DIM)

    def angles(p, n_dims, theta):
        freqs = theta ** (-jnp.arange(0, n_dims, 2, dtype=F32) / n_dims)
        ang = p[:, None] * freqs[None, :]
        return jnp.concatenate([ang, ang], axis=-1)

    half = HEAD_DIM // 2
    row = jnp.floor(pos / GRID_W)
    col = pos - row * GRID_W
    ang = jnp.concatenate([angles(row, half, AXIAL_THETA), angles(col, half, AXIAL_THETA)], axis=-1)
    first = (lane % half) < (half // 2)
    ax_c = jnp.cos(ang)
    ax_sa = jnp.where(first[None, :], -jnp.sin(ang), 0.0)
    ax_sb = jnp.where(first[None, :], 0.0, jnp.sin(ang))

    angp = angles(pos, ROPE_DIMS, ROPE_THETA)
    pad = jnp.zeros((n_pos, HEAD_DIM - ROPE_DIMS), F32)
    cosp = jnp.concatenate([jnp.cos(angp), pad + 1.0], axis=-1)
    sinp = jnp.concatenate([jnp.sin(angp), pad], axis=-1)
    p_first = lane < (ROPE_DIMS // 2)
    p_second = (lane >= ROPE_DIMS // 2) & (lane < ROPE_DIMS)
    pr_c = cosp
    pr_sa = jnp.where(p_first[None, :], -sinp, 0.0)
    pr_sb = jnp.where(p_second[None, :], sinp, 0.0)

    two = lambda t: jnp.concatenate([t, t], axis=-1).astype(F32)
    return tuple(two(t) for t in (ax_c, ax_sa, ax_sb, pr_c, pr_sa, pr_sb))


def _head_mean_matrix():
    i = jnp.arange(LANES)
    same = (i[:, None] // HEAD_DIM) == (i[None, :] // HEAD_DIM)
    return jnp.where(same, 1.0 / HEAD_DIM, 0.0).astype(BF16)


def _rms_rows(x, g):
    return x * lax.rsqrt(jnp.mean(x * x, axis=-1, keepdims=True) + EPS) * g


def _head_norm_rope(x, gmat, gain, c, sa, sb, half, scale):
    ss = jnp.dot((x * x).astype(BF16), gmat, preferred_element_type=F32)
    y = x * lax.rsqrt(ss + EPS) * gain
    out = y * c + pltpu.roll(y, LANES - half, 1) * sa + pltpu.roll(y, half, 1) * sb
    if scale != 1.0:
        out = out * scale
    return out.astype(BF16)


def _proj0_kernel(x_ref, g_ref, w_ref, gmat_ref, aq_ref, ak_ref, bq_ref, bk_ref,
                  axc_ref, axsa_ref, axsb_ref, prc_ref, prsa_ref, prsb_ref,
                  qa_ref, ka_ref, va_ref, qb_ref, kb_ref, vb_ref, sg_ref):
    h = _rms_rows(x_ref[...], g_ref[...]).astype(BF16)
    gmat = gmat_ref[...]
    ax = (axc_ref[...], axsa_ref[...], axsb_ref[...])
    pr = (prc_ref[...], prsa_ref[...], prsb_ref[...])

    def proj(c0, width):
        return jnp.dot(h, w_ref[:, c0:c0 + width], preferred_element_type=F32)

    def normed(z, out_ref, gain, tabs, half, scale):
        for c in range(z.shape[1] // LANES):
            sl = slice(c * LANES, (c + 1) * LANES)
            out_ref[:, sl] = _head_norm_rope(z[:, sl], gmat, gain, *tabs, half, scale)

    c0 = 0
    normed(proj(c0, A_Q_W), qa_ref, aq_ref[...], ax, HEAD_DIM // 4, QK_SCALE)
    c0 += A_Q_W
    zkv = proj(c0, 2 * A_KV_W)
    normed(zkv[:, :A_KV_W], ka_ref, ak_ref[...], ax, HEAD_DIM // 4, 1.0)
    va_ref[...] = zkv[:, A_KV_W:].astype(BF16)
    c0 += 2 * A_KV_W
    normed(proj(c0, B_QK_W), qb_ref, bq_ref[...], pr, ROPE_DIMS // 2, QK_SCALE)
    c0 += B_QK_W
    normed(proj(c0, B_QK_W), kb_ref, bk_ref[...], pr, ROPE_DIMS // 2, 1.0)
    c0 += B_QK_W
    vb_ref[...] = proj(c0, B_V_W).astype(BF16)
    c0 += B_V_W
    gate = proj(c0, EVEN_MIX_W)
    sg_ref[...] = (gate * jax.nn.sigmoid(gate)).astype(BF16)


def _row_spec(width, rows=PROJ_ROWS):
    return pl.BlockSpec((rows, width), lambda i: (i, 0))


def _const_spec(shape):
    return pl.BlockSpec(shape, lambda i: (0,) * len(shape))


def _proj0(x2d, seq, norm_g, w_bf16, gmat, gains, tables):
    n = x2d.shape[0]
    pos_blocks = seq // PROJ_ROWS
    tab_spec = pl.BlockSpec((PROJ_ROWS, LANES), lambda i: (i % pos_blocks, 0))
    widths = (A_Q_W, A_KV_W, A_KV_W, B_QK_W, B_QK_W, B_V_W, EVEN_MIX_W)
    return pl.pallas_call(
        _proj0_kernel,
        grid=(n // PROJ_ROWS,),
        in_specs=[_row_spec(D_MODEL), _const_spec((1, D_MODEL)), _const_spec(w_bf16.shape),
                  _const_spec((LANES, LANES))] + [_const_spec((1, LANES))] * 4 + [tab_spec] * 6,
        out_specs=[_row_spec(w) for w in widths],
        out_shape=[jax.ShapeDtypeStruct((n, w), BF16) for w in widths],
        compiler_params=pltpu.CompilerParams(dimension_semantics=("parallel",),
                                             vmem_limit_bytes=VMEM_LIMIT),
        name="proj0",
    )(x2d, norm_g, w_bf16, gmat, *gains, *tables)


def _softmax_parts(q, k):
    s = lax.dot_general(q, k, (((1,), (1,)), ((), ())), preferred_element_type=F32)
    m = jnp.max(s, axis=-1, keepdims=True)
    e = jnp.exp(s - m)
    return e, jnp.sum(e, axis=-1, keepdims=True)


def _attn0_kernel(lam_init, lq1_ref, lk1_ref, lq2_ref, lk2_ref, subln_ref,
                  qa_ref, qb_ref, sg_ref, x_ref, ka_ref, va_ref, kb_ref, vb_ref, w_ref, y_ref):
    lam = (jnp.exp(jnp.sum(lq1_ref[...] * lk1_ref[...], axis=-1, keepdims=True))
           - jnp.exp(jnp.sum(lq2_ref[...] * lk2_ref[...], axis=-1, keepdims=True)) + lam_init)

    outs = []
    group = A_Q_HEADS // A_KV_HEADS
    for g in range(A_KV_HEADS):
        k = ka_ref[:, g * HEAD_DIM:(g + 1) * HEAD_DIM]
        v = va_ref[:, g * HEAD_DIM:(g + 1) * HEAD_DIM]
        for j in range(group):
            hq = g * group + j
            e, l = _softmax_parts(qa_ref[:, hq * HEAD_DIM:(hq + 1) * HEAD_DIM], k)
            o = jnp.dot(e.astype(BF16), v, preferred_element_type=F32)
            outs.append(o * (1.0 / l))

    subln = subln_ref[...]
    for hb in range(B_HEADS):
        c0 = hb * 2 * HEAD_DIM
        e0, l0 = _softmax_parts(qb_ref[:, c0:c0 + HEAD_DIM], kb_ref[:, c0:c0 + HEAD_DIM])
        e1, l1 = _softmax_parts(qb_ref[:, c0 + HEAD_DIM:c0 + 2 * HEAD_DIM],
                                kb_ref[:, c0 + HEAD_DIM:c0 + 2 * HEAD_DIM])
        a = (e0 * (1.0 / l0) - e1 * (lam / l1)).astype(BF16)
        o = jnp.dot(a, vb_ref[:, c0:c0 + 2 * HEAD_DIM], preferred_element_type=F32)
        outs.append(_rms_rows(o, subln) * (1.0 - lam_init))

    mix = (jnp.concatenate(outs, axis=-1) * sg_ref[...].astype(F32)).astype(BF16)
    y_ref[...] = x_ref[...] + jnp.dot(mix, w_ref[...], preferred_element_type=F32)


def _attn0(x3d, qa, ka, va, qb, kb, vb, sg, w_out, lams, subln, lam_init):
    b, s, _ = x3d.shape
    r3 = lambda t: t.reshape(b, s, t.shape[-1])
    qa, ka, va, qb, kb, vb, sg = map(r3, (qa, ka, va, qb, kb, vb, sg))
    row = lambda w: pl.BlockSpec((None, ATTN_ROWS, w), lambda bi, i: (bi, i, 0))
    full = lambda w: pl.BlockSpec((None, s, w), lambda bi, i: (bi, 0, 0))
    const = lambda shape: pl.BlockSpec(shape, lambda bi, i: (0,) * len(shape))
    return pl.pallas_call(
        functools.partial(_attn0_kernel, lam_init),
        grid=(b, s // ATTN_ROWS),
        in_specs=[const((1, HEAD_DIM))] * 4 + [const((1, 2 * HEAD_DIM)),
                  row(A_Q_W), row(B_QK_W), row(EVEN_MIX_W), row(D_MODEL),
                  full(A_KV_W), full(A_KV_W), full(B_QK_W), full(B_V_W), const(w_out.shape)],
        out_specs=row(D_MODEL),
        out_shape=jax.ShapeDtypeStruct((b, s, D_MODEL), F32),
        compiler_params=pltpu.CompilerParams(dimension_semantics=("parallel", "arbitrary"),
                                             vmem_limit_bytes=VMEM_LIMIT),
        name="attn0",
    )(*lams, subln, qa, qb, sg, x3d, ka, va, kb, vb, w_out)


def _proj1_kernel(x_ref, g_ref, w_ref, gmat_ref, cq_ref, ck_ref, prc_ref, prsa_ref, prsb_ref,
                  q_ref, k_ref, v_ref, sg_ref):
    h = _rms_rows(x_ref[...], g_ref[...]).astype(BF16)
    gmat = gmat_ref[...]
    pr = (prc_ref[...], prsa_ref[...], prsb_ref[...])

    def proj(c0, width):
        return jnp.dot(h, w_ref[:, c0:c0 + width], preferred_element_type=F32)

    for out_ref, gain_ref, c0, scale in ((q_ref, cq_ref, 0, QK_SCALE), (k_ref, ck_ref, C_W, 1.0)):
        gain = gain_ref[...]
        for half_w in range(2):
            z = proj(c0 + half_w * (C_W // 2), C_W // 2)
            for c in range(C_W // 2 // LANES):
                sl = slice(c * LANES, (c + 1) * LANES)
                out_ref[:, half_w * (C_W // 2) + c * LANES: half_w * (C_W // 2) + (c + 1) * LANES] = (
                    _head_norm_rope(z[:, sl], gmat, gain, *pr, ROPE_DIMS // 2, scale))
    v_ref[...] = proj(2 * C_W, C_W).astype(BF16)
    gate = proj(3 * C_W, C_W)
    sg_ref[...] = (gate * jax.nn.sigmoid(gate)).astype(BF16)


def _proj1(x2d, seq, norm_g, w_bf16, gmat, gains, tables):
    n = x2d.shape[0]
    pos_blocks = seq // PROJ_ROWS
    tab_spec = pl.BlockSpec((PROJ_ROWS, LANES), lambda i: (i % pos_blocks, 0))
    return pl.pallas_call(
        _proj1_kernel,
        grid=(n // PROJ_ROWS,),
        in_specs=[_row_spec(D_MODEL), _const_spec((1, D_MODEL)), _const_spec(w_bf16.shape),
                  _const_spec((LANES, LANES))] + [_const_spec((1, LANES))] * 2 + [tab_spec] * 3,
        out_specs=[_row_spec(C_W)] * 4,
        out_shape=[jax.ShapeDtypeStruct((n, C_W), BF16)] * 4,
        compiler_params=pltpu.CompilerParams(dimension_semantics=("parallel",),
                                             vmem_limit_bytes=VMEM_LIMIT),
        name="proj1",
    )(x2d, norm_g, w_bf16, gmat, *gains, *tables)


def _dilated_kernel(seq, q_ref, k_ref, v_ref, o_ref, qf, kf, vf, qd, kd, vd, bias, acc, mx, den):
    qf[...] = q_ref[...].astype(F32)
    kf[...] = k_ref[...].astype(F32)
    vf[...] = v_ref[...].astype(F32)

    lane = lax.broadcasted_iota(jnp.int32, (1, LANES), 1)
    head0 = lane < HEAD_DIM
    rr = lax.broadcasted_iota(jnp.int32, (BAND_ROWS, 2 * BAND_ROWS), 0)
    cc = lax.broadcasted_iota(jnp.int32, (BAND_ROWS, 2 * BAND_ROWS), 1)
    band = (cc >= rr) & (cc <= rr + 2 * BAND_RADIUS)
    not_before = cc >= BAND_RADIUS
    not_after = cc < BAND_ROWS + BAND_RADIUS
    for idx, ok in enumerate((band, band & not_before, band & not_after, band & not_before & not_after)):
        bias[idx] = jnp.where(ok, 0.0, NEG).astype(F32)
    zeros_pad = jnp.zeros((BAND_RADIUS, LANES), BF16)
    ones_cols = jnp.ones((2 * BAND_ROWS, LANES), BF16)

    for p, (_, dil) in enumerate(C_PATTERNS):
        length = seq // dil
        padded = length + 2 * BAND_RADIUS
        blocks = length // BAND_ROWS
        for r in range(dil):
            rows = pl.ds(r, length, stride=dil) if dil > 1 else pl.ds(0, length)
            qd[r * length:(r + 1) * length, :] = qf[rows, :].astype(BF16)
            kd[r * padded:r * padded + BAND_RADIUS, :] = zeros_pad
            kd[r * padded + BAND_RADIUS:r * padded + BAND_RADIUS + length, :] = kf[rows, :].astype(BF16)
            kd[r * padded + BAND_RADIUS + length:(r + 1) * padded, :] = zeros_pad
            vd[r * padded:r * padded + BAND_RADIUS, :] = zeros_pad
            vd[r * padded + BAND_RADIUS:r * padded + BAND_RADIUS + length, :] = vf[rows, :].astype(BF16)
            vd[r * padded + BAND_RADIUS + length:(r + 1) * padded, :] = zeros_pad

        def block(t, carry, dil=dil, length=length, padded=padded, blocks=blocks, p=p):
            r = t // blocks
            i = t - r * blocks
            q0 = pl.multiple_of(r * length + i * BAND_ROWS, BAND_ROWS)
            k0 = pl.multiple_of(r * padded + i * BAND_ROWS, BAND_RADIUS)
            qb = qd[pl.ds(q0, BAND_ROWS), :]
            kb = kd[pl.ds(k0, 2 * BAND_ROWS), :]
            vb = vd[pl.ds(k0, 2 * BAND_ROWS), :]
            which = (i == 0).astype(jnp.int32) + 2 * (i == blocks - 1).astype(jnp.int32)
            bias_blk = bias[which]
            zq = jnp.zeros_like(qb)
            qs = jnp.concatenate([jnp.where(head0, qb, zq), jnp.where(head0, zq, qb)], axis=0)
            s = lax.dot_general(qs, kb, (((1,), (1,)), ((), ())), preferred_element_type=F32)
            s = s + jnp.concatenate([bias_blk, bias_blk], axis=0)
            m = jnp.max(s, axis=-1, keepdims=True)
            e = jnp.exp(s - m).astype(BF16)
            o2 = jnp.dot(e, jnp.concatenate([vb, ones_cols], axis=1), preferred_element_type=F32)
            mb = jnp.broadcast_to(m, (2 * BAND_ROWS, LANES))
            o = jnp.where(head0, o2[:BAND_ROWS, :LANES], o2[BAND_ROWS:, :LANES])
            l2 = jnp.where(head0, o2[:BAND_ROWS, LANES:], o2[BAND_ROWS:, LANES:])
            m2 = jnp.where(head0, mb[:BAND_ROWS], mb[BAND_ROWS:])
            if dil > 1:
                dst = pl.ds(i * (BAND_ROWS * dil) + r, BAND_ROWS, stride=dil)
            else:
                dst = pl.ds(q0, BAND_ROWS)
            acc[p, dst, :] = o
            mx[p, dst, :] = m2
            den[p, dst, :] = l2
            return carry

        lax.fori_loop(0, dil * blocks, block, 0, unroll=BLOCK_UNROLL)

    m_all = jnp.maximum(jnp.maximum(mx[0], mx[1]), mx[2])
    num = jnp.zeros((seq, LANES), F32)
    dsum = jnp.zeros((seq, LANES), F32)
    for p in range(len(C_PATTERNS)):
        w = jnp.exp(mx[p] - m_all)
        num = num + w * acc[p]
        dsum = dsum + w * den[p]
    o_ref[...] = (num / dsum).astype(BF16)


def _dilated(q, k, v, b, s):
    r3 = lambda t: t.reshape(b, s, C_W)
    spec = pl.BlockSpec((None, s, LANES), lambda bi, hp: (bi, 0, hp))
    n_pat = len(C_PATTERNS)
    max_dil = max(d for _, d in C_PATTERNS)
    padded_rows = s + 2 * BAND_RADIUS * max_dil
    return pl.pallas_call(
        functools.partial(_dilated_kernel, s),
        grid=(b, C_W // LANES),
        in_specs=[spec] * 3,
        out_specs=spec,
        out_shape=jax.ShapeDtypeStruct((b, s, C_W), BF16),
        scratch_shapes=[pltpu.VMEM((s, LANES), F32)] * 3
        + [pltpu.VMEM((s, LANES), BF16)]
        + [pltpu.VMEM((padded_rows, LANES), BF16)] * 2
        + [pltpu.VMEM((4, BAND_ROWS, 2 * BAND_ROWS), F32)]
        + [pltpu.VMEM((n_pat, s, LANES), F32)] * 3,
        compiler_params=pltpu.CompilerParams(dimension_semantics=("parallel", "arbitrary"),
                                             vmem_limit_bytes=VMEM_LIMIT),
        name="dilated",
    )(r3(q), r3(k), r3(v))


def _out1_kernel(o_ref, sg_ref, x_ref, w_ref, y_ref):
    mix = (o_ref[...].astype(F32) * sg_ref[...].astype(F32)).astype(BF16)
    y_ref[...] = x_ref[...] + jnp.dot(mix, w_ref[...], preferred_element_type=F32)


def _out1(o2d, sg, x2d, w_bf16):
    n = x2d.shape[0]
    return pl.pallas_call(
        _out1_kernel,
        grid=(n // PROJ_ROWS,),
        in_specs=[_row_spec(C_W), _row_spec(C_W), _row_spec(D_MODEL), _const_spec(w_bf16.shape)],
        out_specs=_row_spec(D_MODEL),
        out_shape=jax.ShapeDtypeStruct((n, D_MODEL), F32),
        compiler_params=pltpu.CompilerParams(dimension_semantics=("parallel",),
                                             vmem_limit_bytes=VMEM_LIMIT),
        name="out1",
    )(o2d, sg, x2d, w_bf16)


def _trunk(x, p):
    b, s, _ = x.shape
    x2d = x.reshape(b * s, D_MODEL)
    qa, ka, va, qb, kb, vb, sg = _proj0(x2d, s, p["norm0"], p["w_in0"], p["gmat"], p["gains0"], p["tables"])
    y0 = _attn0(x, qa, ka, va, qb, kb, vb, sg, p["w_out0"], p["lams"], p["subln"], p["lam_init"])
    y0_2d = y0.reshape(b * s, D_MODEL)
    q, k, v, sg1 = _proj1(y0_2d, s, p["norm1"], p["w_in1"], p["gmat"], p["gains1"], p["tables"][3:])
    o = _dilated(q, k, v, b, s)
    y1 = _out1(o.reshape(b * s, C_W), sg1, y0_2d, p["w_out1"])
    return y1.reshape(b, s, D_MODEL)


def kernel(x_prompt, x_sample, norm0, w_in0, w_out0, a_q_norm, a_k_norm, b_q_norm, b_k_norm, lambda_q1, lambda_k1, lambda_q2, lambda_k2, b_subln, norm1, w_in1, w_out1, c_q_norm, c_k_norm):
    two = lambda g: jnp.concatenate([g, g]).reshape(1, LANES).astype(F32)
    max_seq = max(x_prompt.shape[1], x_sample.shape[1])
    params = {
        "norm0": norm0.reshape(1, D_MODEL), "norm1": norm1.reshape(1, D_MODEL),
        "w_in0": w_in0.astype(BF16), "w_out0": w_out0.astype(BF16),
        "w_in1": w_in1.astype(BF16), "w_out1": w_out1.astype(BF16),
        "gmat": _head_mean_matrix(),
        "gains0": tuple(two(g) for g in (a_q_norm, a_k_norm, b_q_norm, b_k_norm)),
        "gains1": tuple(two(g) for g in (c_q_norm, c_k_norm)),
        "tables": _rope_tables(max_seq),
        "lams": tuple(t.reshape(1, HEAD_DIM) for t in (lambda_q1, lambda_k1, lambda_q2, lambda_k2)),
        "subln": b_subln.reshape(1, 2 * HEAD_DIM),
        "lam_init": 0.8 - 0.6 * math.exp(-0.3 * 0),
    }
    return (_trunk(x_prompt, params), _trunk(x_sample, params))
```

```python
import functools
import math

import jax
import jax.numpy as jnp
from jax import lax
from jax.experimental import pallas as pl
from jax.experimental.pallas import tpu as pltpu

D_MODEL = 1024
HEAD_DIM = 64
GRID_W = 64
EPS = 1e-6
NEG = -1e30

A_Q_HEADS = 8
A_KV_HEADS = 2
AXIAL_THETA = 10000.0
B_HEADS = 4
C_HEADS = 16
C_PATTERNS = ((128, 1), (512, 4), (2048, 16))
ROPE_THETA = 500000.0
ROPE_DIMS = HEAD_DIM // 4

A_Q_W = A_Q_HEADS * HEAD_DIM
A_KV_W = A_KV_HEADS * HEAD_DIM
B_QK_W = B_HEADS * 2 * HEAD_DIM
B_V_W = B_HEADS * 2 * HEAD_DIM
EVEN_MIX_W = A_Q_W + B_V_W
C_W = C_HEADS * HEAD_DIM

LANES = 128
VMEM_LIMIT = 56 * 1024 * 1024
PROJ_ROWS = 512
ATTN_ROWS = 256
BAND_ROWS = 128
BAND_RADIUS = 64
BLOCK_UNROLL = 4
BLOCK_UNROLL_SMALL = 8
SAFE_SCORE = 30.0
NORM_SLACK = 1.02
QK_SCALE = HEAD_DIM ** -0.5

BF16 = jnp.bfloat16
F32 = jnp.float32


def _rope_tables(n_pos):
    pos = jnp.arange(n_pos, dtype=F32)
    lane = jnp.arange(HEAD_DIM)

    def angles(p, n_dims, theta):
        freqs = theta ** (-jnp.arange(0, n_dims, 2, dtype=F32) / n_dims)
        ang = p[:, None] * freqs[None, :]
        return jnp.concatenate([ang, ang], axis=-1)

    half = HEAD_DIM // 2
    row = jnp.floor(pos / GRID_W)
    col = pos - row * GRID_W
    ang = jnp.concatenate([angles(row, half, AXIAL_THETA), angles(col, half, AXIAL_THETA)], axis=-1)
    first = (lane % half) < (half // 2)
    ax_c = jnp.cos(ang)
    ax_sa = jnp.where(first[None, :], -jnp.sin(ang), 0.0)
    ax_sb = jnp.where(first[None, :], 0.0, jnp.sin(ang))

    angp = angles(pos, ROPE_DIMS, ROPE_THETA)
    pad = jnp.zeros((n_pos, HEAD_DIM - ROPE_DIMS), F32)
    cosp = jnp.concatenate([jnp.cos(angp), pad + 1.0], axis=-1)
    sinp = jnp.concatenate([jnp.sin(angp), pad], axis=-1)
    p_first = lane < (ROPE_DIMS // 2)
    p_second = (lane >= ROPE_DIMS // 2) & (lane < ROPE_DIMS)
    pr_c = cosp
    pr_sa = jnp.where(p_first[None, :], -sinp, 0.0)
    pr_sb = jnp.where(p_second[None, :], sinp, 0.0)

    two = lambda t: jnp.concatenate([t, t], axis=-1).astype(F32)
    return tuple(two(t) for t in (ax_c, ax_sa, ax_sb, pr_c, pr_sa, pr_sb))


def _scores_are_small(q_gain, k_gain):
    bound = (HEAD_DIM ** 0.5) * NORM_SLACK * jnp.max(jnp.abs(q_gain)) * jnp.max(jnp.abs(k_gain))
    return (bound <= SAFE_SCORE).astype(jnp.int32).reshape(1)


def _head_mean_matrix():
    i = jnp.arange(LANES)
    same = (i[:, None] // HEAD_DIM) == (i[None, :] // HEAD_DIM)
    return jnp.where(same, 1.0 / HEAD_DIM, 0.0).astype(BF16)


def _rms_rows(x, g):
    return x * lax.rsqrt(jnp.mean(x * x, axis=-1, keepdims=True) + EPS) * g


def _head_norm_rope(x, gmat, gain, c, sa, sb, half, scale):
    ss = jnp.dot((x * x).astype(BF16), gmat, preferred_element_type=F32)
    y = x * lax.rsqrt(ss + EPS) * gain
    out = y * c + pltpu.roll(y, LANES - half, 1) * sa + pltpu.roll(y, half, 1) * sb
    if scale != 1.0:
        out = out * scale
    return out.astype(BF16)


def _proj0_kernel(x_ref, g_ref, w_ref, gmat_ref, aq_ref, ak_ref, bq_ref, bk_ref,
                  axc_ref, axsa_ref, axsb_ref, prc_ref, prsa_ref, prsb_ref,
                  qa_ref, ka_ref, va_ref, qb_ref, kb_ref, vb_ref, sg_ref):
    h = _rms_rows(x_ref[...], g_ref[...]).astype(BF16)
    gmat = gmat_ref[...]
    ax = (axc_ref[...], axsa_ref[...], axsb_ref[...])
    pr = (prc_ref[...], prsa_ref[...], prsb_ref[...])

    def proj(c0, width):
        return jnp.dot(h, w_ref[:, c0:c0 + width], preferred_element_type=F32)

    def normed(z, out_ref, gain, tabs, half, scale):
        for c in range(z.shape[1] // LANES):
            sl = slice(c * LANES, (c + 1) * LANES)
            out_ref[:, sl] = _head_norm_rope(z[:, sl], gmat, gain, *tabs, half, scale)

    c0 = 0
    normed(proj(c0, A_Q_W), qa_ref, aq_ref[...], ax, HEAD_DIM // 4, QK_SCALE)
    c0 += A_Q_W
    zkv = proj(c0, 2 * A_KV_W)
    normed(zkv[:, :A_KV_W], ka_ref, ak_ref[...], ax, HEAD_DIM // 4, 1.0)
    va_ref[...] = zkv[:, A_KV_W:].astype(BF16)
    c0 += 2 * A_KV_W
    normed(proj(c0, B_QK_W), qb_ref, bq_ref[...], pr, ROPE_DIMS // 2, QK_SCALE)
    c0 += B_QK_W
    normed(proj(c0, B_QK_W), kb_ref, bk_ref[...], pr, ROPE_DIMS // 2, 1.0)
    c0 += B_QK_W
    vb_ref[...] = proj(c0, B_V_W).astype(BF16)
    c0 += B_V_W
    gate = proj(c0, EVEN_MIX_W)
    sg_ref[...] = (gate * jax.nn.sigmoid(gate)).astype(BF16)


def _row_spec(width, rows=PROJ_ROWS):
    return pl.BlockSpec((rows, width), lambda i: (i, 0))


def _const_spec(shape):
    return pl.BlockSpec(shape, lambda i: (0,) * len(shape))


def _proj0(x2d, seq, norm_g, w_bf16, gmat, gains, tables):
    n = x2d.shape[0]
    pos_blocks = seq // PROJ_ROWS
    tab_spec = pl.BlockSpec((PROJ_ROWS, LANES), lambda i: (i % pos_blocks, 0))
    widths = (A_Q_W, A_KV_W, A_KV_W, B_QK_W, B_QK_W, B_V_W, EVEN_MIX_W)
    return pl.pallas_call(
        _proj0_kernel,
        grid=(n // PROJ_ROWS,),
        in_specs=[_row_spec(D_MODEL), _const_spec((1, D_MODEL)), _const_spec(w_bf16.shape),
                  _const_spec((LANES, LANES))] + [_const_spec((1, LANES))] * 4 + [tab_spec] * 6,
        out_specs=[_row_spec(w) for w in widths],
        out_shape=[jax.ShapeDtypeStruct((n, w), BF16) for w in widths],
        compiler_params=pltpu.CompilerParams(dimension_semantics=("parallel",),
                                             vmem_limit_bytes=VMEM_LIMIT),
        name="proj0",
    )(x2d, norm_g, w_bf16, gmat, *gains, *tables)


def _exp_scores(q, k, small):
    s = lax.dot_general(q, k, (((1,), (1,)), ((), ())), preferred_element_type=F32)
    if not small:
        s = s - jnp.max(s, axis=-1, keepdims=True)
    return jnp.exp(s).astype(BF16)


def _attn0_body(lam_init, small, lq1_ref, lk1_ref, lq2_ref, lk2_ref, subln_ref,
                qa_ref, qb_ref, sg_ref, x_ref, ka_ref, va_ref, kb_ref, vb_ref, w_ref, y_ref):
    lam = (jnp.exp(jnp.sum(lq1_ref[...] * lk1_ref[...], axis=-1, keepdims=True))
           - jnp.exp(jnp.sum(lq2_ref[...] * lk2_ref[...], axis=-1, keepdims=True)) + lam_init)
    ones_cols = jnp.ones((ka_ref.shape[0], LANES), BF16)

    outs = []
    group = A_Q_HEADS // A_KV_HEADS
    va_ext = jnp.concatenate([va_ref[...], ones_cols], axis=1)
    for g in range(A_KV_HEADS):
        k = ka_ref[:, g * HEAD_DIM:(g + 1) * HEAD_DIM]
        for j in range(group):
            hq = g * group + j
            e = _exp_scores(qa_ref[:, hq * HEAD_DIM:(hq + 1) * HEAD_DIM], k, small)
            o2 = jnp.dot(e, va_ext, preferred_element_type=F32)
            outs.append(o2[:, g * HEAD_DIM:(g + 1) * HEAD_DIM] * (1.0 / o2[:, LANES:LANES + HEAD_DIM]))

    subln = subln_ref[...]
    for hb in range(B_HEADS):
        c0 = hb * 2 * HEAD_DIM
        vb_ext = jnp.concatenate([vb_ref[:, c0:c0 + 2 * HEAD_DIM], ones_cols], axis=1)
        parts = []
        for c in range(2):
            cols = slice(c0 + c * HEAD_DIM, c0 + (c + 1) * HEAD_DIM)
            parts.append(jnp.dot(_exp_scores(qb_ref[:, cols], kb_ref[:, cols], small), vb_ext,
                                 preferred_element_type=F32))
        o = (parts[0][:, :LANES] * (1.0 / parts[0][:, LANES:])
             - parts[1][:, :LANES] * (lam / parts[1][:, LANES:]))
        outs.append(_rms_rows(o, subln) * (1.0 - lam_init))

    mix = (jnp.concatenate(outs, axis=-1) * sg_ref[...].astype(F32)).astype(BF16)
    y_ref[...] = x_ref[...] + jnp.dot(mix, w_ref[...], preferred_element_type=F32)


def _attn0_kernel(lam_init, small_ref, *refs):
    small = small_ref[0] > 0
    pl.when(small)(lambda: _attn0_body(lam_init, True, *refs))
    pl.when(jnp.logical_not(small))(lambda: _attn0_body(lam_init, False, *refs))


def _attn0(small, x3d, qa, ka, va, qb, kb, vb, sg, w_out, lams, subln, lam_init):
    b, s, _ = x3d.shape
    r3 = lambda t: t.reshape(b, s, t.shape[-1])
    qa, ka, va, qb, kb, vb, sg = map(r3, (qa, ka, va, qb, kb, vb, sg))
    row = lambda w: pl.BlockSpec((None, ATTN_ROWS, w), lambda bi, i: (bi, i, 0))
    full = lambda w: pl.BlockSpec((None, s, w), lambda bi, i: (bi, 0, 0))
    const = lambda shape: pl.BlockSpec(shape, lambda bi, i: (0,) * len(shape))
    return pl.pallas_call(
        functools.partial(_attn0_kernel, lam_init),
        grid=(b, s // ATTN_ROWS),
        in_specs=[pl.BlockSpec(memory_space=pltpu.SMEM)] + [const((1, HEAD_DIM))] * 4 + [const((1, 2 * HEAD_DIM)),
                  row(A_Q_W), row(B_QK_W), row(EVEN_MIX_W), row(D_MODEL),
                  full(A_KV_W), full(A_KV_W), full(B_QK_W), full(B_V_W), const(w_out.shape)],
        out_specs=row(D_MODEL),
        out_shape=jax.ShapeDtypeStruct((b, s, D_MODEL), F32),
        compiler_params=pltpu.CompilerParams(dimension_semantics=("parallel", "arbitrary"),
                                             vmem_limit_bytes=VMEM_LIMIT),
        name="attn0",
    )(small, *lams, subln, qa, qb, sg, x3d, ka, va, kb, vb, w_out)


def _proj1_kernel(x_ref, g_ref, w_ref, gmat_ref, cq_ref, ck_ref, prc_ref, prsa_ref, prsb_ref,
                  q_ref, k_ref, v_ref, sg_ref):
    h = _rms_rows(x_ref[...], g_ref[...]).astype(BF16)
    gmat = gmat_ref[...]
    pr = (prc_ref[...], prsa_ref[...], prsb_ref[...])

    def proj(c0, width):
        return jnp.dot(h, w_ref[:, c0:c0 + width], preferred_element_type=F32)

    for out_ref, gain_ref, c0, scale in ((q_ref, cq_ref, 0, QK_SCALE), (k_ref, ck_ref, C_W, 1.0)):
        gain = gain_ref[...]
        for half_w in range(2):
            z = proj(c0 + half_w * (C_W // 2), C_W // 2)
            for c in range(C_W // 2 // LANES):
                sl = slice(c * LANES, (c + 1) * LANES)
                out_ref[:, half_w * (C_W // 2) + c * LANES: half_w * (C_W // 2) + (c + 1) * LANES] = (
                    _head_norm_rope(z[:, sl], gmat, gain, *pr, ROPE_DIMS // 2, scale))
    v_ref[...] = proj(2 * C_W, C_W).astype(BF16)
    gate = proj(3 * C_W, C_W)
    sg_ref[...] = (gate * jax.nn.sigmoid(gate)).astype(BF16)


def _proj1(x2d, seq, norm_g, w_bf16, gmat, gains, tables):
    n = x2d.shape[0]
    pos_blocks = seq // PROJ_ROWS
    tab_spec = pl.BlockSpec((PROJ_ROWS, LANES), lambda i: (i % pos_blocks, 0))
    return pl.pallas_call(
        _proj1_kernel,
        grid=(n // PROJ_ROWS,),
        in_specs=[_row_spec(D_MODEL), _const_spec((1, D_MODEL)), _const_spec(w_bf16.shape),
                  _const_spec((LANES, LANES))] + [_const_spec((1, LANES))] * 2 + [tab_spec] * 3,
        out_specs=[_row_spec(C_W)] * 4,
        out_shape=[jax.ShapeDtypeStruct((n, C_W), BF16)] * 4,
        compiler_params=pltpu.CompilerParams(dimension_semantics=("parallel",),
                                             vmem_limit_bytes=VMEM_LIMIT),
        name="proj1",
    )(x2d, norm_g, w_bf16, gmat, *gains, *tables)


def _dilated_kernel(seq, small_ref, q_ref, k_ref, v_ref, o_ref, qf, kf, vf, qd, kd, vd, bias, acc, mx, den):
    qf[...] = q_ref[...].astype(F32)
    kf[...] = k_ref[...].astype(F32)
    vf[...] = v_ref[...].astype(F32)

    rr = lax.broadcasted_iota(jnp.int32, (BAND_ROWS, 2 * BAND_ROWS), 0)
    cc = lax.broadcasted_iota(jnp.int32, (BAND_ROWS, 2 * BAND_ROWS), 1)
    band = (cc >= rr) & (cc <= rr + 2 * BAND_RADIUS)
    not_before = cc >= BAND_RADIUS
    not_after = cc < BAND_ROWS + BAND_RADIUS
    for idx, ok in enumerate((band, band & not_before, band & not_after, band & not_before & not_after)):
        bias[idx] = jnp.where(ok, 0.0, NEG).astype(F32)

    refs = (o_ref, qf, kf, vf, qd, kd, vd, bias, acc, mx, den)
    run_patterns = functools.partial(_dilated_patterns, seq, refs=refs)
    small = small_ref[0] > 0
    pl.when(small)(lambda: run_patterns(True))
    pl.when(jnp.logical_not(small))(lambda: run_patterns(False))


def _dilated_patterns(seq, small, refs):
    o_ref, qf, kf, vf, qd, kd, vd, bias, acc, mx, den = refs
    lane = lax.broadcasted_iota(jnp.int32, (1, LANES), 1)
    head0 = lane < HEAD_DIM
    zeros_pad = jnp.zeros((BAND_RADIUS, LANES), BF16)
    ones_cols = jnp.ones((2 * BAND_ROWS, LANES), BF16)

    for p, (_, dil) in enumerate(C_PATTERNS):
        length = seq // dil
        padded = length + 2 * BAND_RADIUS
        blocks = length // BAND_ROWS
        for r in range(dil):
            rows = pl.ds(r, length, stride=dil) if dil > 1 else pl.ds(0, length)
            qd[r * length:(r + 1) * length, :] = qf[rows, :].astype(BF16)
            kd[r * padded:r * padded + BAND_RADIUS, :] = zeros_pad
            kd[r * padded + BAND_RADIUS:r * padded + BAND_RADIUS + length, :] = kf[rows, :].astype(BF16)
            kd[r * padded + BAND_RADIUS + length:(r + 1) * padded, :] = zeros_pad
            vd[r * padded:r * padded + BAND_RADIUS, :] = zeros_pad
            vd[r * padded + BAND_RADIUS:r * padded + BAND_RADIUS + length, :] = vf[rows, :].astype(BF16)
            vd[r * padded + BAND_RADIUS + length:(r + 1) * padded, :] = zeros_pad

        def block(t, carry, dil=dil, length=length, padded=padded, blocks=blocks, p=p):
            r = t // blocks
            i = t - r * blocks
            q0 = pl.multiple_of(r * length + i * BAND_ROWS, BAND_ROWS)
            k0 = pl.multiple_of(r * padded + i * BAND_ROWS, BAND_RADIUS)
            qb = qd[pl.ds(q0, BAND_ROWS), :]
            kb = kd[pl.ds(k0, 2 * BAND_ROWS), :]
            vb = vd[pl.ds(k0, 2 * BAND_ROWS), :]
            which = (i == 0).astype(jnp.int32) + 2 * (i == blocks - 1).astype(jnp.int32)
            bias_blk = bias[which]
            zq = jnp.zeros_like(qb)
            qs = jnp.concatenate([jnp.where(head0, qb, zq), jnp.where(head0, zq, qb)], axis=0)
            s = lax.dot_general(qs, kb, (((1,), (1,)), ((), ())), preferred_element_type=F32)
            s = s + jnp.concatenate([bias_blk, bias_blk], axis=0)
            if small:
                e = jnp.exp(s).astype(BF16)
            else:
                m = jnp.max(s, axis=-1, keepdims=True)
                e = jnp.exp(s - m).astype(BF16)
            o2 = jnp.dot(e, jnp.concatenate([vb, ones_cols], axis=1), preferred_element_type=F32)
            o = jnp.where(head0, o2[:BAND_ROWS, :LANES], o2[BAND_ROWS:, :LANES])
            l2 = jnp.where(head0, o2[:BAND_ROWS, LANES:], o2[BAND_ROWS:, LANES:])
            if dil > 1:
                dst = pl.ds(i * (BAND_ROWS * dil) + r, BAND_ROWS, stride=dil)
            else:
                dst = pl.ds(q0, BAND_ROWS)
            acc[p, dst, :] = o
            den[p, dst, :] = l2
            if not small:
                mb = jnp.broadcast_to(m, (2 * BAND_ROWS, LANES))
                mx[p, dst, :] = jnp.where(head0, mb[:BAND_ROWS], mb[BAND_ROWS:])
            return carry

        lax.fori_loop(0, dil * blocks, block, 0, unroll=BLOCK_UNROLL_SMALL if small else BLOCK_UNROLL)

    num = jnp.zeros((seq, LANES), F32)
    dsum = jnp.zeros((seq, LANES), F32)
    if small:
        for p in range(len(C_PATTERNS)):
            num = num + acc[p]
            dsum = dsum + den[p]
    else:
        m_all = jnp.maximum(jnp.maximum(mx[0], mx[1]), mx[2])
        for p in range(len(C_PATTERNS)):
            w = jnp.exp(mx[p] - m_all)
            num = num + w * acc[p]
            dsum = dsum + w * den[p]
    o_ref[...] = (num / dsum).astype(BF16)


def _dilated(small, q, k, v, b, s):
    r3 = lambda t: t.reshape(b, s, C_W)
    spec = pl.BlockSpec((None, s, LANES), lambda bi, hp: (bi, 0, hp))
    n_pat = len(C_PATTERNS)
    max_dil = max(d for _, d in C_PATTERNS)
    padded_rows = s + 2 * BAND_RADIUS * max_dil
    return pl.pallas_call(
        functools.partial(_dilated_kernel, s),
        grid=(b, C_W // LANES),
        in_specs=[pl.BlockSpec(memory_space=pltpu.SMEM)] + [spec] * 3,
        out_specs=spec,
        out_shape=jax.ShapeDtypeStruct((b, s, C_W), BF16),
        scratch_shapes=[pltpu.VMEM((s, LANES), F32)] * 3
        + [pltpu.VMEM((s, LANES), BF16)]
        + [pltpu.VMEM((padded_rows, LANES), BF16)] * 2
        + [pltpu.VMEM((4, BAND_ROWS, 2 * BAND_ROWS), F32)]
        + [pltpu.VMEM((n_pat, s, LANES), F32)] * 3,
        compiler_params=pltpu.CompilerParams(dimension_semantics=("parallel", "arbitrary"),
                                             vmem_limit_bytes=VMEM_LIMIT),
        name="dilated",
    )(small, r3(q), r3(k), r3(v))


def _out1_kernel(o_ref, sg_ref, x_ref, w_ref, y_ref):
    mix = (o_ref[...].astype(F32) * sg_ref[...].astype(F32)).astype(BF16)
    y_ref[...] = x_ref[...] + jnp.dot(mix, w_ref[...], preferred_element_type=F32)


def _out1(o2d, sg, x2d, w_bf16):
    n = x2d.shape[0]
    return pl.pallas_call(
        _out1_kernel,
        grid=(n // PROJ_ROWS,),
        in_specs=[_row_spec(C_W), _row_spec(C_W), _row_spec(D_MODEL), _const_spec(w_bf16.shape)],
        out_specs=_row_spec(D_MODEL),
        out_shape=jax.ShapeDtypeStruct((n, D_MODEL), F32),
        compiler_params=pltpu.CompilerParams(dimension_semantics=("parallel",),
                                             vmem_limit_bytes=VMEM_LIMIT),
        name="out1",
    )(o2d, sg, x2d, w_bf16)


def _trunk(x, p):
    b, s, _ = x.shape
    x2d = x.reshape(b * s, D_MODEL)
    qa, ka, va, qb, kb, vb, sg = _proj0(x2d, s, p["norm0"], p["w_in0"], p["gmat"], p["gains0"], p["tables"])
    y0 = _attn0(p["small0"], x, qa, ka, va, qb, kb, vb, sg, p["w_out0"], p["lams"], p["subln"], p["lam_init"])
    y0_2d = y0.reshape(b * s, D_MODEL)
    q, k, v, sg1 = _proj1(y0_2d, s, p["norm1"], p["w_in1"], p["gmat"], p["gains1"], p["tables"][3:])
    o = _dilated(p["small1"], q, k, v, b, s)
    y1 = _out1(o.reshape(b * s, C_W), sg1, y0_2d, p["w_out1"])
    return y1.reshape(b, s, D_MODEL)


def kernel(x_prompt, x_sample, norm0, w_in0, w_out0, a_q_norm, a_k_norm, b_q_norm, b_k_norm, lambda_q1, lambda_k1, lambda_q2, lambda_k2, b_subln, norm1, w_in1, w_out1, c_q_norm, c_k_norm):
    two = lambda g: jnp.concatenate([g, g]).reshape(1, LANES).astype(F32)
    max_seq = max(x_prompt.shape[1], x_sample.shape[1])
    params = {
        "norm0": norm0.reshape(1, D_MODEL), "norm1": norm1.reshape(1, D_MODEL),
        "w_in0": w_in0.astype(BF16), "w_out0": w_out0.astype(BF16),
        "w_in1": w_in1.astype(BF16), "w_out1": w_out1.astype(BF16),
        "gmat": _head_mean_matrix(),
        "gains0": tuple(two(g) for g in (a_q_norm, a_k_norm, b_q_norm, b_k_norm)),
        "gains1": tuple(two(g) for g in (c_q_norm, c_k_norm)),
        "tables": _rope_tables(max_seq),
        "lams": tuple(t.reshape(1, HEAD_DIM) for t in (lambda_q1, lambda_k1, lambda_q2, lambda_k2)),
        "subln": b_subln.reshape(1, 2 * HEAD_DIM),
        "lam_init": 0.8 - 0.6 * math.exp(-0.3 * 0),
        "small0": _scores_are_small(a_q_norm, a_k_norm) * _scores_are_small(b_q_norm, b_k_norm),
        "small1": _scores_are_small(c_q_norm, c_k_norm),
    }
    return (_trunk(x_prompt, params), _trunk(x_sample, params))
```

```python
import functools
import math

import jax
import jax.numpy as jnp
from jax import lax
from jax.experimental import pallas as pl
from jax.experimental.pallas import tpu as pltpu

D_MODEL = 1024
HEAD_DIM = 64
GRID_W = 64
EPS = 1e-6
NEG = -1e30

A_Q_HEADS = 8
A_KV_HEADS = 2
AXIAL_THETA = 10000.0
B_HEADS = 4
C_HEADS = 16
C_PATTERNS = ((128, 1), (512, 4), (2048, 16))
ROPE_THETA = 500000.0
ROPE_DIMS = HEAD_DIM // 4

A_Q_W = A_Q_HEADS * HEAD_DIM
A_KV_W = A_KV_HEADS * HEAD_DIM
B_QK_W = B_HEADS * 2 * HEAD_DIM
B_V_W = B_HEADS * 2 * HEAD_DIM
EVEN_MIX_W = A_Q_W + B_V_W
C_W = C_HEADS * HEAD_DIM

LANES = 128
VMEM_LIMIT = 56 * 1024 * 1024
PROJ_ROWS = 512
ATTN_ROWS = 256
BAND_ROWS = 128
BAND_RADIUS = 64
BLOCK_UNROLL = 4
BLOCK_UNROLL_SMALL = 32
SAFE_SCORE = 30.0
NORM_SLACK = 1.02
QK_SCALE = HEAD_DIM ** -0.5

BF16 = jnp.bfloat16
F32 = jnp.float32


def _rope_tables(n_pos):
    pos = jnp.arange(n_pos, dtype=F32)
    lane = jnp.arange(HEAD_DIM)

    def angles(p, n_dims, theta):
        freqs = theta ** (-jnp.arange(0, n_dims, 2, dtype=F32) / n_dims)
        ang = p[:, None] * freqs[None, :]
        return jnp.concatenate([ang, ang], axis=-1)

    half = HEAD_DIM // 2
    row = jnp.floor(pos / GRID_W)
    col = pos - row * GRID_W
    ang = jnp.concatenate([angles(row, half, AXIAL_THETA), angles(col, half, AXIAL_THETA)], axis=-1)
    first = (lane % half) < (half // 2)
    ax_c = jnp.cos(ang)
    ax_sa = jnp.where(first[None, :], -jnp.sin(ang), 0.0)
    ax_sb = jnp.where(first[None, :], 0.0, jnp.sin(ang))

    angp = angles(pos, ROPE_DIMS, ROPE_THETA)
    pad = jnp.zeros((n_pos, HEAD_DIM - ROPE_DIMS), F32)
    cosp = jnp.concatenate([jnp.cos(angp), pad + 1.0], axis=-1)
    sinp = jnp.concatenate([jnp.sin(angp), pad], axis=-1)
    p_first = lane < (ROPE_DIMS // 2)
    p_second = (lane >= ROPE_DIMS // 2) & (lane < ROPE_DIMS)
    pr_c = cosp
    pr_sa = jnp.where(p_first[None, :], -sinp, 0.0)
    pr_sb = jnp.where(p_second[None, :], sinp, 0.0)

    two = lambda t: jnp.concatenate([t, t], axis=-1).astype(F32)
    return tuple(two(t) for t in (ax_c, ax_sa, ax_sb, pr_c, pr_sa, pr_sb))


def _scores_are_small(q_gain, k_gain):
    bound = (HEAD_DIM ** 0.5) * NORM_SLACK * jnp.max(jnp.abs(q_gain)) * jnp.max(jnp.abs(k_gain))
    return (bound <= SAFE_SCORE).astype(jnp.int32).reshape(1)


def _head_mean_matrix():
    i = jnp.arange(LANES)
    same = (i[:, None] // HEAD_DIM) == (i[None, :] // HEAD_DIM)
    return jnp.where(same, 1.0 / HEAD_DIM, 0.0).astype(BF16)


def _rms_rows(x, g):
    return x * lax.rsqrt(jnp.mean(x * x, axis=-1, keepdims=True) + EPS) * g


def _head_norm_rope(x, gmat, gain, c, sa, sb, half, scale):
    ss = jnp.dot((x * x).astype(BF16), gmat, preferred_element_type=F32)
    y = x * lax.rsqrt(ss + EPS) * gain
    out = y * c + pltpu.roll(y, LANES - half, 1) * sa + pltpu.roll(y, half, 1) * sb
    if scale != 1.0:
        out = out * scale
    return out.astype(BF16)


def _proj0_kernel(x_ref, g_ref, w_ref, gmat_ref, aq_ref, ak_ref, bq_ref, bk_ref,
                  axc_ref, axsa_ref, axsb_ref, prc_ref, prsa_ref, prsb_ref,
                  qa_ref, ka_ref, va_ref, qb_ref, kb_ref, vb_ref, sg_ref):
    h = _rms_rows(x_ref[...], g_ref[...]).astype(BF16)
    gmat = gmat_ref[...]
    ax = (axc_ref[...], axsa_ref[...], axsb_ref[...])
    pr = (prc_ref[...], prsa_ref[...], prsb_ref[...])

    def proj(c0, width):
        return jnp.dot(h, w_ref[:, c0:c0 + width], preferred_element_type=F32)

    def normed(z, out_ref, gain, tabs, half, scale):
        for c in range(z.shape[1] // LANES):
            sl = slice(c * LANES, (c + 1) * LANES)
            out_ref[:, sl] = _head_norm_rope(z[:, sl], gmat, gain, *tabs, half, scale)

    c0 = 0
    normed(proj(c0, A_Q_W), qa_ref, aq_ref[...], ax, HEAD_DIM // 4, QK_SCALE)
    c0 += A_Q_W
    zkv = proj(c0, 2 * A_KV_W)
    normed(zkv[:, :A_KV_W], ka_ref, ak_ref[...], ax, HEAD_DIM // 4, 1.0)
    va_ref[...] = zkv[:, A_KV_W:].astype(BF16)
    c0 += 2 * A_KV_W
    normed(proj(c0, B_QK_W), qb_ref, bq_ref[...], pr, ROPE_DIMS // 2, QK_SCALE)
    c0 += B_QK_W
    normed(proj(c0, B_QK_W), kb_ref, bk_ref[...], pr, ROPE_DIMS // 2, 1.0)
    c0 += B_QK_W
    vb_ref[...] = proj(c0, B_V_W).astype(BF16)
    c0 += B_V_W
    gate = proj(c0, EVEN_MIX_W)
    sg_ref[...] = (gate * jax.nn.sigmoid(gate)).astype(BF16)


def _row_spec(width, rows=PROJ_ROWS):
    return pl.BlockSpec((rows, width), lambda i: (i, 0))


def _const_spec(shape):
    return pl.BlockSpec(shape, lambda i: (0,) * len(shape))


def _proj0(x2d, seq, norm_g, w_bf16, gmat, gains, tables):
    n = x2d.shape[0]
    pos_blocks = seq // PROJ_ROWS
    tab_spec = pl.BlockSpec((PROJ_ROWS, LANES), lambda i: (i % pos_blocks, 0))
    widths = (A_Q_W, A_KV_W, A_KV_W, B_QK_W, B_QK_W, B_V_W, EVEN_MIX_W)
    return pl.pallas_call(
        _proj0_kernel,
        grid=(n // PROJ_ROWS,),
        in_specs=[_row_spec(D_MODEL), _const_spec((1, D_MODEL)), _const_spec(w_bf16.shape),
                  _const_spec((LANES, LANES))] + [_const_spec((1, LANES))] * 4 + [tab_spec] * 6,
        out_specs=[_row_spec(w) for w in widths],
        out_shape=[jax.ShapeDtypeStruct((n, w), BF16) for w in widths],
        compiler_params=pltpu.CompilerParams(dimension_semantics=("parallel",),
                                             vmem_limit_bytes=VMEM_LIMIT),
        name="proj0",
    )(x2d, norm_g, w_bf16, gmat, *gains, *tables)


def _exp_scores(q, k, small):
    s = lax.dot_general(q, k, (((1,), (1,)), ((), ())), preferred_element_type=F32)
    if not small:
        s = s - jnp.max(s, axis=-1, keepdims=True)
    return jnp.exp(s).astype(BF16)


def _attn0_body(lam_init, small, lq1_ref, lk1_ref, lq2_ref, lk2_ref, subln_ref,
                qa_ref, qb_ref, sg_ref, x_ref, ka_ref, va_ref, kb_ref, vb_ref, w_ref, y_ref):
    lam = (jnp.exp(jnp.sum(lq1_ref[...] * lk1_ref[...], axis=-1, keepdims=True))
           - jnp.exp(jnp.sum(lq2_ref[...] * lk2_ref[...], axis=-1, keepdims=True)) + lam_init)
    ones_cols = jnp.ones((ka_ref.shape[0], LANES), BF16)

    outs = []
    group = A_Q_HEADS // A_KV_HEADS
    va_ext = jnp.concatenate([va_ref[...], ones_cols], axis=1)
    for g in range(A_KV_HEADS):
        k = ka_ref[:, g * HEAD_DIM:(g + 1) * HEAD_DIM]
        for j in range(group):
            hq = g * group + j
            e = _exp_scores(qa_ref[:, hq * HEAD_DIM:(hq + 1) * HEAD_DIM], k, small)
            o2 = jnp.dot(e, va_ext, preferred_element_type=F32)
            outs.append(o2[:, g * HEAD_DIM:(g + 1) * HEAD_DIM] * (1.0 / o2[:, LANES:LANES + HEAD_DIM]))

    subln = subln_ref[...]
    for hb in range(B_HEADS):
        c0 = hb * 2 * HEAD_DIM
        vb_ext = jnp.concatenate([vb_ref[:, c0:c0 + 2 * HEAD_DIM], ones_cols], axis=1)
        parts = []
        for c in range(2):
            cols = slice(c0 + c * HEAD_DIM, c0 + (c + 1) * HEAD_DIM)
            parts.append(jnp.dot(_exp_scores(qb_ref[:, cols], kb_ref[:, cols], small), vb_ext,
                                 preferred_element_type=F32))
        o = (parts[0][:, :LANES] * (1.0 / parts[0][:, LANES:])
             - parts[1][:, :LANES] * (lam / parts[1][:, LANES:]))
        outs.append(_rms_rows(o, subln) * (1.0 - lam_init))

    mix = (jnp.concatenate(outs, axis=-1) * sg_ref[...].astype(F32)).astype(BF16)
    y_ref[...] = x_ref[...] + jnp.dot(mix, w_ref[...], preferred_element_type=F32)


def _attn0(small, lam_init, x3d, qa, ka, va, qb, kb, vb, sg, w_out, lams, subln):
    b, s, _ = x3d.shape
    r3 = lambda t: t.reshape(b, s, t.shape[-1])
    qa, ka, va, qb, kb, vb, sg = map(r3, (qa, ka, va, qb, kb, vb, sg))
    row = lambda w: pl.BlockSpec((None, ATTN_ROWS, w), lambda bi, i: (bi, i, 0))
    full = lambda w: pl.BlockSpec((None, s, w), lambda bi, i: (bi, 0, 0))
    const = lambda shape: pl.BlockSpec(shape, lambda bi, i: (0,) * len(shape))
    return pl.pallas_call(
        functools.partial(_attn0_body, lam_init, small),
        grid=(b, s // ATTN_ROWS),
        in_specs=[const((1, HEAD_DIM))] * 4 + [const((1, 2 * HEAD_DIM)),
                  row(A_Q_W), row(B_QK_W), row(EVEN_MIX_W), row(D_MODEL),
                  full(A_KV_W), full(A_KV_W), full(B_QK_W), full(B_V_W), const(w_out.shape)],
        out_specs=row(D_MODEL),
        out_shape=jax.ShapeDtypeStruct((b, s, D_MODEL), F32),
        compiler_params=pltpu.CompilerParams(dimension_semantics=("parallel", "arbitrary"),
                                             vmem_limit_bytes=VMEM_LIMIT),
        name="attn0" if small else "attn0_rowmax",
    )(*lams, subln, qa, qb, sg, x3d, ka, va, kb, vb, w_out)


def _proj1_kernel(x_ref, g_ref, w_ref, gmat_ref, cq_ref, ck_ref, prc_ref, prsa_ref, prsb_ref,
                  q_ref, k_ref, v_ref, sg_ref):
    h = _rms_rows(x_ref[...], g_ref[...]).astype(BF16)
    gmat = gmat_ref[...]
    pr = (prc_ref[...], prsa_ref[...], prsb_ref[...])

    def proj(c0, width):
        return jnp.dot(h, w_ref[:, c0:c0 + width], preferred_element_type=F32)

    for out_ref, gain_ref, c0, scale in ((q_ref, cq_ref, 0, QK_SCALE), (k_ref, ck_ref, C_W, 1.0)):
        gain = gain_ref[...]
        for half_w in range(2):
            z = proj(c0 + half_w * (C_W // 2), C_W // 2)
            for c in range(C_W // 2 // LANES):
                sl = slice(c * LANES, (c + 1) * LANES)
                out_ref[:, half_w * (C_W // 2) + c * LANES: half_w * (C_W // 2) + (c + 1) * LANES] = (
                    _head_norm_rope(z[:, sl], gmat, gain, *pr, ROPE_DIMS // 2, scale))
    v_ref[...] = proj(2 * C_W, C_W).astype(BF16)
    gate = proj(3 * C_W, C_W)
    sg_ref[...] = (gate * jax.nn.sigmoid(gate)).astype(BF16)


def _proj1(x2d, seq, norm_g, w_bf16, gmat, gains, tables):
    n = x2d.shape[0]
    pos_blocks = seq // PROJ_ROWS
    tab_spec = pl.BlockSpec((PROJ_ROWS, LANES), lambda i: (i % pos_blocks, 0))
    return pl.pallas_call(
        _proj1_kernel,
        grid=(n // PROJ_ROWS,),
        in_specs=[_row_spec(D_MODEL), _const_spec((1, D_MODEL)), _const_spec(w_bf16.shape),
                  _const_spec((LANES, LANES))] + [_const_spec((1, LANES))] * 2 + [tab_spec] * 3,
        out_specs=[_row_spec(C_W)] * 4,
        out_shape=[jax.ShapeDtypeStruct((n, C_W), BF16)] * 4,
        compiler_params=pltpu.CompilerParams(dimension_semantics=("parallel",),
                                             vmem_limit_bytes=VMEM_LIMIT),
        name="proj1",
    )(x2d, norm_g, w_bf16, gmat, *gains, *tables)


def _dilated_kernel(seq, small, q_ref, k_ref, v_ref, o_ref, qf, kf, vf, qd, kd, vd, bias, acc, den, mx=None):
    qf[...] = q_ref[...].astype(F32)
    kf[...] = k_ref[...].astype(F32)
    vf[...] = v_ref[...].astype(F32)

    rr = lax.broadcasted_iota(jnp.int32, (BAND_ROWS, 2 * BAND_ROWS), 0)
    cc = lax.broadcasted_iota(jnp.int32, (BAND_ROWS, 2 * BAND_ROWS), 1)
    band = (cc >= rr) & (cc <= rr + 2 * BAND_RADIUS)
    not_before = cc >= BAND_RADIUS
    not_after = cc < BAND_ROWS + BAND_RADIUS
    for idx, ok in enumerate((band, band & not_before, band & not_after, band & not_before & not_after)):
        bias[idx] = jnp.where(ok, 0.0, NEG).astype(F32)

    lane = lax.broadcasted_iota(jnp.int32, (1, LANES), 1)
    head0 = lane < HEAD_DIM
    zeros_pad = jnp.zeros((BAND_RADIUS, LANES), BF16)
    ones_cols = jnp.ones((2 * BAND_ROWS, LANES), BF16)

    for p, (_, dil) in enumerate(C_PATTERNS):
        length = seq // dil
        padded = length + 2 * BAND_RADIUS
        blocks = length // BAND_ROWS
        q_src = q_ref if dil == 1 else qd
        for r in range(dil):
            if dil == 1:
                k_rows, v_rows = k_ref[...], v_ref[...]
            else:
                rows = pl.ds(r, length, stride=dil)
                qd[r * length:(r + 1) * length, :] = qf[rows, :].astype(BF16)
                k_rows, v_rows = kf[rows, :].astype(BF16), vf[rows, :].astype(BF16)
            kd[r * padded:r * padded + BAND_RADIUS, :] = zeros_pad
            kd[r * padded + BAND_RADIUS:r * padded + BAND_RADIUS + length, :] = k_rows
            kd[r * padded + BAND_RADIUS + length:(r + 1) * padded, :] = zeros_pad
            vd[r * padded:r * padded + BAND_RADIUS, :] = zeros_pad
            vd[r * padded + BAND_RADIUS:r * padded + BAND_RADIUS + length, :] = v_rows
            vd[r * padded + BAND_RADIUS + length:(r + 1) * padded, :] = zeros_pad

        def block(t, carry, dil=dil, length=length, padded=padded, blocks=blocks, p=p, q_src=q_src):
            r = t // blocks
            i = t - r * blocks
            q0 = pl.multiple_of(r * length + i * BAND_ROWS, BAND_ROWS)
            k0 = pl.multiple_of(r * padded + i * BAND_ROWS, BAND_RADIUS)
            qb = q_src[pl.ds(q0, BAND_ROWS), :]
            kb = kd[pl.ds(k0, 2 * BAND_ROWS), :]
            vb = vd[pl.ds(k0, 2 * BAND_ROWS), :]
            which = jnp.where(i == 0, 1, 0) + jnp.where(i == blocks - 1, 2, 0)
            bias_blk = bias[which]
            zq = jnp.zeros_like(qb)
            qs = jnp.concatenate([jnp.where(head0, qb, zq), jnp.where(head0, zq, qb)], axis=0)
            s = lax.dot_general(qs, kb, (((1,), (1,)), ((), ())), preferred_element_type=F32)
            s = s + jnp.concatenate([bias_blk, bias_blk], axis=0)
            if small:
                e = jnp.exp(s).astype(BF16)
            else:
                m = jnp.max(s, axis=-1, keepdims=True)
                e = jnp.exp(s - m).astype(BF16)
            o2 = jnp.dot(e, jnp.concatenate([vb, ones_cols], axis=1), preferred_element_type=F32)
            o = jnp.where(head0, o2[:BAND_ROWS, :LANES], o2[BAND_ROWS:, :LANES])
            l2 = jnp.where(head0, o2[:BAND_ROWS, LANES:], o2[BAND_ROWS:, LANES:])
            if dil > 1:
                dst = pl.ds(i * (BAND_ROWS * dil) + r, BAND_ROWS, stride=dil)
            else:
                dst = pl.ds(q0, BAND_ROWS)
            acc[p, dst, :] = o
            den[p, dst, :] = l2
            if not small:
                mb = jnp.broadcast_to(m, (2 * BAND_ROWS, LANES))
                mx[p, dst, :] = jnp.where(head0, mb[:BAND_ROWS], mb[BAND_ROWS:])
            return carry

        lax.fori_loop(0, dil * blocks, block, 0, unroll=BLOCK_UNROLL_SMALL if small else BLOCK_UNROLL)

    num = jnp.zeros((seq, LANES), F32)
    dsum = jnp.zeros((seq, LANES), F32)
    if small:
        for p in range(len(C_PATTERNS)):
            num = num + acc[p]
            dsum = dsum + den[p]
    else:
        m_all = jnp.maximum(jnp.maximum(mx[0], mx[1]), mx[2])
        for p in range(len(C_PATTERNS)):
            w = jnp.exp(mx[p] - m_all)
            num = num + w * acc[p]
            dsum = dsum + w * den[p]
    o_ref[...] = (num / dsum).astype(BF16)


def _dilated(small, q, k, v, b, s):
    r3 = lambda t: t.reshape(b, s, C_W)
    spec = pl.BlockSpec((None, s, LANES), lambda bi, hp: (bi, 0, hp))
    n_pat = len(C_PATTERNS)
    max_dil = max(d for _, d in C_PATTERNS)
    padded_rows = s + 2 * BAND_RADIUS * max_dil
    return pl.pallas_call(
        functools.partial(_dilated_kernel, s, small),
        grid=(b, C_W // LANES),
        in_specs=[spec] * 3,
        out_specs=spec,
        out_shape=jax.ShapeDtypeStruct((b, s, C_W), BF16),
        scratch_shapes=[pltpu.VMEM((s, LANES), F32)] * 3
        + [pltpu.VMEM((s, LANES), BF16)]
        + [pltpu.VMEM((padded_rows, LANES), BF16)] * 2
        + [pltpu.VMEM((4, BAND_ROWS, 2 * BAND_ROWS), F32)]
        + [pltpu.VMEM((n_pat, s, LANES), F32)] * (2 if small else 3),
        compiler_params=pltpu.CompilerParams(dimension_semantics=("parallel", "arbitrary"),
                                             vmem_limit_bytes=VMEM_LIMIT),
        name="dilated" if small else "dilated_rowmax",
    )(r3(q), r3(k), r3(v))


def _out1_kernel(o_ref, sg_ref, x_ref, w_ref, y_ref):
    mix = (o_ref[...].astype(F32) * sg_ref[...].astype(F32)).astype(BF16)
    y_ref[...] = x_ref[...] + jnp.dot(mix, w_ref[...], preferred_element_type=F32)


def _out1(o2d, sg, x2d, w_bf16):
    n = x2d.shape[0]
    return pl.pallas_call(
        _out1_kernel,
        grid=(n // PROJ_ROWS,),
        in_specs=[_row_spec(C_W), _row_spec(C_W), _row_spec(D_MODEL), _const_spec(w_bf16.shape)],
        out_specs=_row_spec(D_MODEL),
        out_shape=jax.ShapeDtypeStruct((n, D_MODEL), F32),
        compiler_params=pltpu.CompilerParams(dimension_semantics=("parallel",),
                                             vmem_limit_bytes=VMEM_LIMIT),
        name="out1",
    )(o2d, sg, x2d, w_bf16)


def _trunk(x, p):
    b, s, _ = x.shape
    x2d = x.reshape(b * s, D_MODEL)
    qa, ka, va, qb, kb, vb, sg = _proj0(x2d, s, p["norm0"], p["w_in0"], p["gmat"], p["gains0"], p["tables"])
    y0 = lax.cond(p["small0"][0] > 0,
                  functools.partial(_attn0, True, p["lam_init"]), functools.partial(_attn0, False, p["lam_init"]),
                  x, qa, ka, va, qb, kb, vb, sg, p["w_out0"], p["lams"], p["subln"])
    y0_2d = y0.reshape(b * s, D_MODEL)
    q, k, v, sg1 = _proj1(y0_2d, s, p["norm1"], p["w_in1"], p["gmat"], p["gains1"], p["tables"][3:])
    o = lax.cond(p["small1"][0] > 0, functools.partial(_dilated, True, b=b, s=s),
                 functools.partial(_dilated, False, b=b, s=s), q, k, v)
    y1 = _out1(o.reshape(b * s, C_W), sg1, y0_2d, p["w_out1"])
    return y1.reshape(b, s, D_MODEL)


def kernel(x_prompt, x_sample, norm0, w_in0, w_out0, a_q_norm, a_k_norm, b_q_norm, b_k_norm, lambda_q1, lambda_k1, lambda_q2, lambda_k2, b_subln, norm1, w_in1, w_out1, c_q_norm, c_k_norm):
    two = lambda g: jnp.concatenate([g, g]).reshape(1, LANES).astype(F32)
    max_seq = max(x_prompt.shape[1], x_sample.shape[1])
    params = {
        "norm0": norm0.reshape(1, D_MODEL), "norm1": norm1.reshape(1, D_MODEL),
        "w_in0": w_in0.astype(BF16), "w_out0": w_out0.astype(BF16),
        "w_in1": w_in1.astype(BF16), "w_out1": w_out1.astype(BF16),
        "gmat": _head_mean_matrix(),
        "gains0": tuple(two(g) for g in (a_q_norm, a_k_norm, b_q_norm, b_k_norm)),
        "gains1": tuple(two(g) for g in (c_q_norm, c_k_norm)),
        "tables": _rope_tables(max_seq),
        "lams": tuple(t.reshape(1, HEAD_DIM) for t in (lambda_q1, lambda_k1, lambda_q2, lambda_k2)),
        "subln": b_subln.reshape(1, 2 * HEAD_DIM),
        "lam_init": 0.8 - 0.6 * math.exp(-0.3 * 0),
        "small0": _scores_are_small(a_q_norm, a_k_norm) * _scores_are_small(b_q_norm, b_k_norm),
        "small1": _scores_are_small(c_q_norm, c_k_norm),
    }
    return (_trunk(x_prompt, params), _trunk(x_sample, params))
```

```python
import functools
import math

import jax
import jax.numpy as jnp
from jax import lax
from jax.experimental import pallas as pl
from jax.experimental.pallas import tpu as pltpu

D_MODEL = 1024
HEAD_DIM = 64
GRID_W = 64
EPS = 1e-6
NEG = -1e30

A_Q_HEADS = 8
A_KV_HEADS = 2
AXIAL_THETA = 10000.0
B_HEADS = 4
C_HEADS = 16
C_PATTERNS = ((128, 1), (512, 4), (2048, 16))
ROPE_THETA = 500000.0
ROPE_DIMS = HEAD_DIM // 4

A_Q_W = A_Q_HEADS * HEAD_DIM
A_KV_W = A_KV_HEADS * HEAD_DIM
B_QK_W = B_HEADS * 2 * HEAD_DIM
B_V_W = B_HEADS * 2 * HEAD_DIM
EVEN_MIX_W = A_Q_W + B_V_W
C_W = C_HEADS * HEAD_DIM

LANES = 128
NORM_W = 256
VMEM_LIMIT = 56 * 1024 * 1024
PROJ_ROWS = 512
ATTN_ROWS = 256
BAND_ROWS = 128
BAND_RADIUS = 64
BLOCK_UNROLL = 4
BLOCK_UNROLL_SMALL = 32
SAFE_SCORE = 30.0
NORM_SLACK = 1.02
QK_SCALE = HEAD_DIM ** -0.5

BF16 = jnp.bfloat16
F32 = jnp.float32


def _rope_tables(n_pos):
    pos = jnp.arange(n_pos, dtype=F32)
    lane = jnp.arange(HEAD_DIM)

    def angles(p, n_dims, theta):
        freqs = theta ** (-jnp.arange(0, n_dims, 2, dtype=F32) / n_dims)
        ang = p[:, None] * freqs[None, :]
        return jnp.concatenate([ang, ang], axis=-1)

    half = HEAD_DIM // 2
    row = jnp.floor(pos / GRID_W)
    col = pos - row * GRID_W
    ang = jnp.concatenate([angles(row, half, AXIAL_THETA), angles(col, half, AXIAL_THETA)], axis=-1)
    first = (lane % half) < (half // 2)
    ax_c = jnp.cos(ang)
    ax_sa = jnp.where(first[None, :], -jnp.sin(ang), 0.0)
    ax_sb = jnp.where(first[None, :], 0.0, jnp.sin(ang))

    angp = angles(pos, ROPE_DIMS, ROPE_THETA)
    pad = jnp.zeros((n_pos, HEAD_DIM - ROPE_DIMS), F32)
    cosp = jnp.concatenate([jnp.cos(angp), pad + 1.0], axis=-1)
    sinp = jnp.concatenate([jnp.sin(angp), pad], axis=-1)
    p_first = lane < (ROPE_DIMS // 2)
    p_second = (lane >= ROPE_DIMS // 2) & (lane < ROPE_DIMS)
    pr_c = cosp
    pr_sa = jnp.where(p_first[None, :], -sinp, 0.0)
    pr_sb = jnp.where(p_second[None, :], sinp, 0.0)

    two = lambda t: jnp.concatenate([t, t], axis=-1).astype(F32)
    return tuple(two(t) for t in (ax_c, ax_sa, ax_sb, pr_c, pr_sa, pr_sb))


def _scores_are_small(q_gain, k_gain):
    bound = (HEAD_DIM ** 0.5) * NORM_SLACK * jnp.max(jnp.abs(q_gain)) * jnp.max(jnp.abs(k_gain))
    return (bound <= SAFE_SCORE).astype(jnp.int32).reshape(1)


def _head_mean_matrix():
    i = jnp.arange(NORM_W)
    same = (i[:, None] // HEAD_DIM) == (i[None, :] // HEAD_DIM)
    return jnp.where(same, 1.0 / HEAD_DIM, 0.0).astype(BF16)


def _rms_rows(x, g):
    return x * lax.rsqrt(jnp.mean(x * x, axis=-1, keepdims=True) + EPS) * g


def _scaled_rows(x_ref, g_ref):
    x = x_ref[...]
    return (x * g_ref[...]).astype(BF16), lax.rsqrt(jnp.mean(x * x, axis=-1, keepdims=True) + EPS)


def _wide(t):
    return jnp.concatenate([t, t], axis=1)


def _head_norm_rope(x, gmat, gain, c, sa, sb, half, scale):
    ss = jnp.dot((x * x).astype(BF16), gmat, preferred_element_type=F32)
    y = x * lax.rsqrt(ss + EPS) * gain
    out = y * c + pltpu.roll(y, NORM_W - half, 1) * sa + pltpu.roll(y, half, 1) * sb
    if scale != 1.0:
        out = out * scale
    return out.astype(BF16)


def _proj0_kernel(x_ref, g_ref, w_ref, gmat_ref, aq_ref, ak_ref, bq_ref, bk_ref,
                  axc_ref, axsa_ref, axsb_ref, prc_ref, prsa_ref, prsb_ref,
                  qa_ref, ka_ref, va_ref, qb_ref, kb_ref, vb_ref, sg_ref):
    h, row_scale = _scaled_rows(x_ref, g_ref)
    gmat = gmat_ref[...]
    ax = tuple(_wide(t[...]) for t in (axc_ref, axsa_ref, axsb_ref))
    pr = tuple(_wide(t[...]) for t in (prc_ref, prsa_ref, prsb_ref))

    def proj(c0, width):
        return jnp.dot(h, w_ref[:, c0:c0 + width], preferred_element_type=F32) * row_scale

    def normed(z, out_ref, gain, tabs, half, scale):
        for c in range(z.shape[1] // NORM_W):
            sl = slice(c * NORM_W, (c + 1) * NORM_W)
            out_ref[:, sl] = _head_norm_rope(z[:, sl], gmat, _wide(gain), *tabs, half, scale)

    c0 = 0
    normed(proj(c0, A_Q_W), qa_ref, aq_ref[...], ax, HEAD_DIM // 4, QK_SCALE)
    c0 += A_Q_W
    zkv = proj(c0, 2 * A_KV_W)
    kv = _head_norm_rope(zkv, gmat, _wide(ak_ref[...]), *ax, HEAD_DIM // 4, 1.0)
    ka_ref[...] = kv[:, :A_KV_W]
    va_ref[...] = zkv[:, A_KV_W:].astype(BF16)
    c0 += 2 * A_KV_W
    normed(proj(c0, B_QK_W), qb_ref, bq_ref[...], pr, ROPE_DIMS // 2, QK_SCALE)
    c0 += B_QK_W
    normed(proj(c0, B_QK_W), kb_ref, bk_ref[...], pr, ROPE_DIMS // 2, 1.0)
    c0 += B_QK_W
    vb_ref[...] = proj(c0, B_V_W).astype(BF16)
    c0 += B_V_W
    gate = proj(c0, EVEN_MIX_W)
    sg_ref[...] = (gate * jax.nn.sigmoid(gate)).astype(BF16)


def _row_spec(width, rows=PROJ_ROWS):
    return pl.BlockSpec((rows, width), lambda i: (i, 0))


def _const_spec(shape):
    return pl.BlockSpec(shape, lambda i: (0,) * len(shape))


def _proj0(x2d, seq, norm_g, w_bf16, gmat, gains, tables):
    n = x2d.shape[0]
    pos_blocks = seq // PROJ_ROWS
    tab_spec = pl.BlockSpec((PROJ_ROWS, LANES), lambda i: (i % pos_blocks, 0))
    widths = (A_Q_W, A_KV_W, A_KV_W, B_QK_W, B_QK_W, B_V_W, EVEN_MIX_W)
    return pl.pallas_call(
        _proj0_kernel,
        grid=(n // PROJ_ROWS,),
        in_specs=[_row_spec(D_MODEL), _const_spec((1, D_MODEL)), _const_spec(w_bf16.shape),
                  _const_spec((NORM_W, NORM_W))] + [_const_spec((1, LANES))] * 4 + [tab_spec] * 6,
        out_specs=[_row_spec(w) for w in widths],
        out_shape=[jax.ShapeDtypeStruct((n, w), BF16) for w in widths],
        compiler_params=pltpu.CompilerParams(dimension_semantics=("parallel",),
                                             vmem_limit_bytes=VMEM_LIMIT),
        name="proj0",
    )(x2d, norm_g, w_bf16, gmat, *gains, *tables)


def _scores(q, k):
    return lax.dot_general(q, k, (((1,), (1,)), ((), ())), preferred_element_type=F32)


def _exp_scores(q, k, small):
    s = _scores(q, k)
    if not small:
        s = s - jnp.max(s, axis=-1, keepdims=True)
    return jnp.exp(s).astype(BF16)


def _attn0_body(lam_init, small, lq1_ref, lk1_ref, lq2_ref, lk2_ref, subln_ref,
                qa_ref, qb_ref, sg_ref, x_ref, ka_ref, va_ref, kb_ref, vb_ref, w_ref, y_ref):
    lam = (jnp.exp(jnp.sum(lq1_ref[...] * lk1_ref[...], axis=-1, keepdims=True))
           - jnp.exp(jnp.sum(lq2_ref[...] * lk2_ref[...], axis=-1, keepdims=True)) + lam_init)
    ones_cols = jnp.ones((ka_ref.shape[0], LANES), BF16)

    outs = []
    group = A_Q_HEADS // A_KV_HEADS
    va_ext = jnp.concatenate([va_ref[...], ones_cols], axis=1)
    for g in range(A_KV_HEADS):
        k = ka_ref[:, g * HEAD_DIM:(g + 1) * HEAD_DIM]
        for j in range(group):
            hq = g * group + j
            e = _exp_scores(qa_ref[:, hq * HEAD_DIM:(hq + 1) * HEAD_DIM], k, small)
            o2 = jnp.dot(e, va_ext, preferred_element_type=F32)
            outs.append(o2[:, g * HEAD_DIM:(g + 1) * HEAD_DIM] * (1.0 / o2[:, LANES:LANES + HEAD_DIM]))

    subln = subln_ref[...]
    for hb in range(B_HEADS):
        c0 = hb * 2 * HEAD_DIM
        maps = [slice(c0 + c * HEAD_DIM, c0 + (c + 1) * HEAD_DIM) for c in range(2)]
        if small:
            ef = [jnp.exp(_scores(qb_ref[:, cols], kb_ref[:, cols])) for cols in maps]
            l0, l1 = [jnp.sum(e, axis=-1, keepdims=True) for e in ef]
            a = ef[0].astype(BF16) - ef[1].astype(BF16) * (lam * l0 / l1).astype(BF16)
            o = jnp.dot(a, vb_ref[:, c0:c0 + 2 * HEAD_DIM], preferred_element_type=F32) * (1.0 / l0)
        else:
            vb_ext = jnp.concatenate([vb_ref[:, c0:c0 + 2 * HEAD_DIM], ones_cols], axis=1)
            parts = [jnp.dot(_exp_scores(qb_ref[:, cols], kb_ref[:, cols], small), vb_ext,
                             preferred_element_type=F32) for cols in maps]
            o = (parts[0][:, :LANES] * (1.0 / parts[0][:, LANES:])
                 - parts[1][:, :LANES] * (lam / parts[1][:, LANES:]))
        outs.append(_rms_rows(o, subln) * (1.0 - lam_init))

    mix = (jnp.concatenate(outs, axis=-1) * sg_ref[...].astype(F32)).astype(BF16)
    y_ref[...] = x_ref[...] + jnp.dot(mix, w_ref[...], preferred_element_type=F32)


def _attn0(small, lam_init, x3d, qa, ka, va, qb, kb, vb, sg, w_out, lams, subln):
    b, s, _ = x3d.shape
    r3 = lambda t: t.reshape(b, s, t.shape[-1])
    qa, ka, va, qb, kb, vb, sg = map(r3, (qa, ka, va, qb, kb, vb, sg))
    row = lambda w: pl.BlockSpec((None, ATTN_ROWS, w), lambda bi, i: (bi, i, 0))
    full = lambda w: pl.BlockSpec((None, s, w), lambda bi, i: (bi, 0, 0))
    const = lambda shape: pl.BlockSpec(shape, lambda bi, i: (0,) * len(shape))
    return pl.pallas_call(
        functools.partial(_attn0_body, lam_init, small),
        grid=(b, s // ATTN_ROWS),
        in_specs=[const((1, HEAD_DIM))] * 4 + [const((1, 2 * HEAD_DIM)),
                  row(A_Q_W), row(B_QK_W), row(EVEN_MIX_W), row(D_MODEL),
                  full(A_KV_W), full(A_KV_W), full(B_QK_W), full(B_V_W), const(w_out.shape)],
        out_specs=row(D_MODEL),
        out_shape=jax.ShapeDtypeStruct((b, s, D_MODEL), F32),
        compiler_params=pltpu.CompilerParams(dimension_semantics=("parallel", "arbitrary"),
                                             vmem_limit_bytes=VMEM_LIMIT),
        name="attn0" if small else "attn0_rowmax",
    )(*lams, subln, qa, qb, sg, x3d, ka, va, kb, vb, w_out)


def _proj1_kernel(x_ref, g_ref, w_ref, gmat_ref, cq_ref, ck_ref, prc_ref, prsa_ref, prsb_ref,
                  q_ref, k_ref, v_ref, sg_ref):
    h, row_scale = _scaled_rows(x_ref, g_ref)
    gmat = gmat_ref[...]
    pr = tuple(_wide(t[...]) for t in (prc_ref, prsa_ref, prsb_ref))

    def proj(c0, width):
        return jnp.dot(h, w_ref[:, c0:c0 + width], preferred_element_type=F32) * row_scale

    for out_ref, gain_ref, c0, scale in ((q_ref, cq_ref, 0, QK_SCALE), (k_ref, ck_ref, C_W, 1.0)):
        gain = _wide(gain_ref[...])
        for half_w in range(2):
            z = proj(c0 + half_w * (C_W // 2), C_W // 2)
            for c in range(C_W // 2 // NORM_W):
                sl = slice(c * NORM_W, (c + 1) * NORM_W)
                out_ref[:, half_w * (C_W // 2) + c * NORM_W: half_w * (C_W // 2) + (c + 1) * NORM_W] = (
                    _head_norm_rope(z[:, sl], gmat, gain, *pr, ROPE_DIMS // 2, scale))
    v_ref[...] = proj(2 * C_W, C_W).astype(BF16)
    gate = proj(3 * C_W, C_W)
    sg_ref[...] = (gate * jax.nn.sigmoid(gate)).astype(BF16)


def _proj1(x2d, seq, norm_g, w_bf16, gmat, gains, tables):
    n = x2d.shape[0]
    pos_blocks = seq // PROJ_ROWS
    tab_spec = pl.BlockSpec((PROJ_ROWS, LANES), lambda i: (i % pos_blocks, 0))
    return pl.pallas_call(
        _proj1_kernel,
        grid=(n // PROJ_ROWS,),
        in_specs=[_row_spec(D_MODEL), _const_spec((1, D_MODEL)), _const_spec(w_bf16.shape),
                  _const_spec((NORM_W, NORM_W))] + [_const_spec((1, LANES))] * 2 + [tab_spec] * 3,
        out_specs=[_row_spec(C_W)] * 4,
        out_shape=[jax.ShapeDtypeStruct((n, C_W), BF16)] * 4,
        compiler_params=pltpu.CompilerParams(dimension_semantics=("parallel",),
                                             vmem_limit_bytes=VMEM_LIMIT),
        name="proj1",
    )(x2d, norm_g, w_bf16, gmat, *gains, *tables)


def _dilated_kernel(seq, small, q_ref, k_ref, v_ref, o_ref, qf, kf, vf, qd, kd, vd, bias, acc, den, mx=None):
    qf[...] = q_ref[...].astype(F32)
    kf[...] = k_ref[...].astype(F32)
    vf[...] = v_ref[...].astype(F32)

    rr = lax.broadcasted_iota(jnp.int32, (BAND_ROWS, 2 * BAND_ROWS), 0)
    cc = lax.broadcasted_iota(jnp.int32, (BAND_ROWS, 2 * BAND_ROWS), 1)
    band = (cc >= rr) & (cc <= rr + 2 * BAND_RADIUS)
    not_before = cc >= BAND_RADIUS
    not_after = cc < BAND_ROWS + BAND_RADIUS
    for idx, ok in enumerate((band, band & not_before, band & not_after, band & not_before & not_after)):
        bias[idx] = jnp.where(ok, 0.0, NEG).astype(F32)

    lane = lax.broadcasted_iota(jnp.int32, (1, LANES), 1)
    head0 = lane < HEAD_DIM
    zeros_pad = jnp.zeros((BAND_RADIUS, LANES), BF16)
    ones_cols = jnp.ones((2 * BAND_ROWS, LANES), BF16)

    for p, (_, dil) in enumerate(C_PATTERNS):
        length = seq // dil
        padded = length + 2 * BAND_RADIUS
        blocks = length // BAND_ROWS
        q_src = q_ref if dil == 1 else qd
        for r in range(dil):
            if dil == 1:
                k_rows, v_rows = k_ref[...], v_ref[...]
            else:
                rows = pl.ds(r, length, stride=dil)
                qd[r * length:(r + 1) * length, :] = qf[rows, :].astype(BF16)
                k_rows, v_rows = kf[rows, :].astype(BF16), vf[rows, :].astype(BF16)
            kd[r * padded:r * padded + BAND_RADIUS, :] = zeros_pad
            kd[r * padded + BAND_RADIUS:r * padded + BAND_RADIUS + length, :] = k_rows
            kd[r * padded + BAND_RADIUS + length:(r + 1) * padded, :] = zeros_pad
            vd[r * padded:r * padded + BAND_RADIUS, :] = zeros_pad
            vd[r * padded + BAND_RADIUS:r * padded + BAND_RADIUS + length, :] = v_rows
            vd[r * padded + BAND_RADIUS + length:(r + 1) * padded, :] = zeros_pad

        def block(t, carry, dil=dil, length=length, padded=padded, blocks=blocks, p=p, q_src=q_src):
            r = t // blocks
            i = t - r * blocks
            q0 = pl.multiple_of(r * length + i * BAND_ROWS, BAND_ROWS)
            k0 = pl.multiple_of(r * padded + i * BAND_ROWS, BAND_RADIUS)
            qb = q_src[pl.ds(q0, BAND_ROWS), :]
            kb = kd[pl.ds(k0, 2 * BAND_ROWS), :]
            vb = vd[pl.ds(k0, 2 * BAND_ROWS), :]
            which = jnp.where(i == 0, 1, 0) + jnp.where(i == blocks - 1, 2, 0)
            bias_blk = bias[which]
            zq = jnp.zeros_like(qb)
            qs = jnp.concatenate([jnp.where(head0, qb, zq), jnp.where(head0, zq, qb)], axis=0)
            s = lax.dot_general(qs, kb, (((1,), (1,)), ((), ())), preferred_element_type=F32)
            s = s + jnp.concatenate([bias_blk, bias_blk], axis=0)
            if small:
                e = jnp.exp(s).astype(BF16)
            else:
                m = jnp.max(s, axis=-1, keepdims=True)
                e = jnp.exp(s - m).astype(BF16)
            o2 = jnp.dot(e, jnp.concatenate([vb, ones_cols], axis=1), preferred_element_type=F32)
            o = jnp.where(head0, o2[:BAND_ROWS, :LANES], o2[BAND_ROWS:, :LANES])
            l2 = jnp.where(head0, o2[:BAND_ROWS, LANES:], o2[BAND_ROWS:, LANES:])
            if dil > 1:
                dst = pl.ds(i * (BAND_ROWS * dil) + r, BAND_ROWS, stride=dil)
            else:
                dst = pl.ds(q0, BAND_ROWS)
            acc[p, dst, :] = o
            den[p, dst, :] = l2
            if not small:
                mb = jnp.broadcast_to(m, (2 * BAND_ROWS, LANES))
                mx[p, dst, :] = jnp.where(head0, mb[:BAND_ROWS], mb[BAND_ROWS:])
            return carry

        lax.fori_loop(0, dil * blocks, block, 0, unroll=BLOCK_UNROLL_SMALL if small else BLOCK_UNROLL)

    num = jnp.zeros((seq, LANES), F32)
    dsum = jnp.zeros((seq, LANES), F32)
    if small:
        for p in range(len(C_PATTERNS)):
            num = num + acc[p]
            dsum = dsum + den[p]
    else:
        m_all = jnp.maximum(jnp.maximum(mx[0], mx[1]), mx[2])
        for p in range(len(C_PATTERNS)):
            w = jnp.exp(mx[p] - m_all)
            num = num + w * acc[p]
            dsum = dsum + w * den[p]
    o_ref[...] = (num / dsum).astype(BF16)


def _dilated(small, q, k, v, b, s):
    r3 = lambda t: t.reshape(b, s, C_W)
    spec = pl.BlockSpec((None, s, LANES), lambda bi, hp: (bi, 0, hp))
    n_pat = len(C_PATTERNS)
    max_dil = max(d for _, d in C_PATTERNS)
    padded_rows = s + 2 * BAND_RADIUS * max_dil
    return pl.pallas_call(
        functools.partial(_dilated_kernel, s, small),
        grid=(b, C_W // LANES),
        in_specs=[spec] * 3,
        out_specs=spec,
        out_shape=jax.ShapeDtypeStruct((b, s, C_W), BF16),
        scratch_shapes=[pltpu.VMEM((s, LANES), F32)] * 3
        + [pltpu.VMEM((s, LANES), BF16)]
        + [pltpu.VMEM((padded_rows, LANES), BF16)] * 2
        + [pltpu.VMEM((4, BAND_ROWS, 2 * BAND_ROWS), F32)]
        + [pltpu.VMEM((n_pat, s, LANES), F32)] * (2 if small else 3),
        compiler_params=pltpu.CompilerParams(dimension_semantics=("parallel", "arbitrary"),
                                             vmem_limit_bytes=VMEM_LIMIT),
        name="dilated" if small else "dilated_rowmax",
    )(r3(q), r3(k), r3(v))


def _out1_kernel(o_ref, sg_ref, x_ref, w_ref, y_ref):
    mix = (o_ref[...].astype(F32) * sg_ref[...].astype(F32)).astype(BF16)
    y_ref[...] = x_ref[...] + jnp.dot(mix, w_ref[...], preferred_element_type=F32)


def _out1(o2d, sg, x2d, w_bf16):
    n = x2d.shape[0]
    return pl.pallas_call(
        _out1_kernel,
        grid=(n // PROJ_ROWS,),
        in_specs=[_row_spec(C_W), _row_spec(C_W), _row_spec(D_MODEL), _const_spec(w_bf16.shape)],
        out_specs=_row_spec(D_MODEL),
        out_shape=jax.ShapeDtypeStruct((n, D_MODEL), F32),
        compiler_params=pltpu.CompilerParams(dimension_semantics=("parallel",),
                                             vmem_limit_bytes=VMEM_LIMIT),
        name="out1",
    )(o2d, sg, x2d, w_bf16)


def _trunk(x, p):
    b, s, _ = x.shape
    x2d = x.reshape(b * s, D_MODEL)
    qa, ka, va, qb, kb, vb, sg = _proj0(x2d, s, p["norm0"], p["w_in0"], p["gmat"], p["gains0"], p["tables"])
    y0 = lax.cond(p["small0"][0] > 0,
                  functools.partial(_attn0, True, p["lam_init"]), functools.partial(_attn0, False, p["lam_init"]),
                  x, qa, ka, va, qb, kb, vb, sg, p["w_out0"], p["lams"], p["subln"])
    y0_2d = y0.reshape(b * s, D_MODEL)
    q, k, v, sg1 = _proj1(y0_2d, s, p["norm1"], p["w_in1"], p["gmat"], p["gains1"], p["tables"][3:])
    o = lax.cond(p["small1"][0] > 0, functools.partial(_dilated, True, b=b, s=s),
                 functools.partial(_dilated, False, b=b, s=s), q, k, v)
    y1 = _out1(o.reshape(b * s, C_W), sg1, y0_2d, p["w_out1"])
    return y1.reshape(b, s, D_MODEL)


def kernel(x_prompt, x_sample, norm0, w_in0, w_out0, a_q_norm, a_k_norm, b_q_norm, b_k_norm, lambda_q1, lambda_k1, lambda_q2, lambda_k2, b_subln, norm1, w_in1, w_out1, c_q_norm, c_k_norm):
    two = lambda g: jnp.concatenate([g, g]).reshape(1, LANES).astype(F32)
    max_seq = max(x_prompt.shape[1], x_sample.shape[1])
    params = {
        "norm0": norm0.reshape(1, D_MODEL), "norm1": norm1.reshape(1, D_MODEL),
        "w_in0": w_in0.astype(BF16), "w_out0": w_out0.astype(BF16),
        "w_in1": w_in1.astype(BF16), "w_out1": w_out1.astype(BF16),
        "gmat": _head_mean_matrix(),
        "gains0": tuple(two(g) for g in (a_q_norm, a_k_norm, b_q_norm, b_k_norm)),
        "gains1": tuple(two(g) for g in (c_q_norm, c_k_norm)),
        "tables": _rope_tables(max_seq),
        "lams": tuple(t.reshape(1, HEAD_DIM) for t in (lambda_q1, lambda_k1, lambda_q2, lambda_k2)),
        "subln": b_subln.reshape(1, 2 * HEAD_DIM),
        "lam_init": 0.8 - 0.6 * math.exp(-0.3 * 0),
        "small0": _scores_are_small(a_q_norm, a_k_norm) * _scores_are_small(b_q_norm, b_k_norm),
        "small1": _scores_are_small(c_q_norm, c_k_norm),
    }
    return (_trunk(x_prompt, params), _trunk(x_sample, params))
```

```python
import functools
import math

import jax
import jax.numpy as jnp
from jax import lax
from jax.experimental import pallas as pl
from jax.experimental.pallas import tpu as pltpu

D_MODEL = 1024
HEAD_DIM = 64
GRID_W = 64
EPS = 1e-6
NEG = -1e30

A_Q_HEADS = 8
A_KV_HEADS = 2
AXIAL_THETA = 10000.0
B_HEADS = 4
C_HEADS = 16
C_PATTERNS = ((128, 1), (512, 4), (2048, 16))
ROPE_THETA = 500000.0
ROPE_DIMS = HEAD_DIM // 4

A_Q_W = A_Q_HEADS * HEAD_DIM
A_KV_W = A_KV_HEADS * HEAD_DIM
B_QK_W = B_HEADS * 2 * HEAD_DIM
B_V_W = B_HEADS * 2 * HEAD_DIM
EVEN_MIX_W = A_Q_W + B_V_W
C_W = C_HEADS * HEAD_DIM

LANES = 128
NORM_W = 256
VMEM_LIMIT = 56 * 1024 * 1024
PROJ_ROWS = 1024
ATTN_SCORE_ELEMS = 512 * 2048
BAND_ROWS = 128
BAND_RADIUS = 64
BLOCK_UNROLL = 4
BLOCK_UNROLL_SMALL = 32
SAFE_SCORE = 30.0
NORM_SLACK = 1.02
QK_SCALE = HEAD_DIM ** -0.5

BF16 = jnp.bfloat16
F32 = jnp.float32


def _rope_tables(n_pos):
    pos = jnp.arange(n_pos, dtype=F32)
    lane = jnp.arange(HEAD_DIM)

    def angles(p, n_dims, theta):
        freqs = theta ** (-jnp.arange(0, n_dims, 2, dtype=F32) / n_dims)
        ang = p[:, None] * freqs[None, :]
        return jnp.concatenate([ang, ang], axis=-1)

    half = HEAD_DIM // 2
    row = jnp.floor(pos / GRID_W)
    col = pos - row * GRID_W
    ang = jnp.concatenate([angles(row, half, AXIAL_THETA), angles(col, half, AXIAL_THETA)], axis=-1)
    first = (lane % half) < (half // 2)
    ax_c = jnp.cos(ang)
    ax_sa = jnp.where(first[None, :], -jnp.sin(ang), 0.0)
    ax_sb = jnp.where(first[None, :], 0.0, jnp.sin(ang))

    angp = angles(pos, ROPE_DIMS, ROPE_THETA)
    pad = jnp.zeros((n_pos, HEAD_DIM - ROPE_DIMS), F32)
    cosp = jnp.concatenate([jnp.cos(angp), pad + 1.0], axis=-1)
    sinp = jnp.concatenate([jnp.sin(angp), pad], axis=-1)
    p_first = lane < (ROPE_DIMS // 2)
    p_second = (lane >= ROPE_DIMS // 2) & (lane < ROPE_DIMS)
    pr_c = cosp
    pr_sa = jnp.where(p_first[None, :], -sinp, 0.0)
    pr_sb = jnp.where(p_second[None, :], sinp, 0.0)

    two = lambda t: jnp.concatenate([t, t], axis=-1).astype(F32)
    return tuple(two(t) for t in (ax_c, ax_sa, ax_sb, pr_c, pr_sa, pr_sb))


def _scores_are_small(q_gain, k_gain):
    bound = (HEAD_DIM ** 0.5) * NORM_SLACK * jnp.max(jnp.abs(q_gain)) * jnp.max(jnp.abs(k_gain))
    return (bound <= SAFE_SCORE).astype(jnp.int32).reshape(1)


def _head_mean_matrix():
    i = jnp.arange(NORM_W)
    same = (i[:, None] // HEAD_DIM) == (i[None, :] // HEAD_DIM)
    return jnp.where(same, 1.0 / HEAD_DIM, 0.0).astype(BF16)


def _rms_rows(x, g):
    return x * lax.rsqrt(jnp.mean(x * x, axis=-1, keepdims=True) + EPS) * g


def _scaled_rows(x_ref, g_ref):
    x = x_ref[...]
    return (x * g_ref[...]).astype(BF16), lax.rsqrt(jnp.mean(x * x, axis=-1, keepdims=True) + EPS)


def _wide(t):
    return jnp.concatenate([t, t], axis=1)


def _head_norm_rope(x, gmat, gain, c, sa, sb, half, scale):
    ss = jnp.dot((x * x).astype(BF16), gmat, preferred_element_type=F32)
    y = x * lax.rsqrt(ss + EPS) * gain
    out = y * c + pltpu.roll(y, NORM_W - half, 1) * sa + pltpu.roll(y, half, 1) * sb
    if scale != 1.0:
        out = out * scale
    return out.astype(BF16)


def _proj0_kernel(x_ref, g_ref, w_ref, gmat_ref, aq_ref, ak_ref, bq_ref, bk_ref,
                  axc_ref, axsa_ref, axsb_ref, prc_ref, prsa_ref, prsb_ref,
                  qa_ref, ka_ref, va_ref, qb_ref, kb_ref, vb_ref, sg_ref):
    h, row_scale = _scaled_rows(x_ref, g_ref)
    gmat = gmat_ref[...]
    ax = tuple(_wide(t[...]) for t in (axc_ref, axsa_ref, axsb_ref))
    pr = tuple(_wide(t[...]) for t in (prc_ref, prsa_ref, prsb_ref))

    def proj(c0, width):
        return jnp.dot(h, w_ref[:, c0:c0 + width], preferred_element_type=F32) * row_scale

    def normed(z, out_ref, gain, tabs, half, scale):
        for c in range(z.shape[1] // NORM_W):
            sl = slice(c * NORM_W, (c + 1) * NORM_W)
            out_ref[:, sl] = _head_norm_rope(z[:, sl], gmat, _wide(gain), *tabs, half, scale)

    c0 = 0
    normed(proj(c0, A_Q_W), qa_ref, aq_ref[...], ax, HEAD_DIM // 4, QK_SCALE)
    c0 += A_Q_W
    zkv = proj(c0, 2 * A_KV_W)
    kv = _head_norm_rope(zkv, gmat, _wide(ak_ref[...]), *ax, HEAD_DIM // 4, 1.0)
    ka_ref[...] = kv[:, :A_KV_W]
    va_ref[...] = zkv[:, A_KV_W:].astype(BF16)
    c0 += 2 * A_KV_W
    normed(proj(c0, B_QK_W), qb_ref, bq_ref[...], pr, ROPE_DIMS // 2, QK_SCALE)
    c0 += B_QK_W
    normed(proj(c0, B_QK_W), kb_ref, bk_ref[...], pr, ROPE_DIMS // 2, 1.0)
    c0 += B_QK_W
    vb_ref[...] = proj(c0, B_V_W).astype(BF16)
    c0 += B_V_W
    gate = proj(c0, EVEN_MIX_W)
    sg_ref[...] = (gate * jax.nn.sigmoid(gate)).astype(BF16)


def _row_spec(width, rows=PROJ_ROWS):
    return pl.BlockSpec((rows, width), lambda i: (i, 0))


def _const_spec(shape):
    return pl.BlockSpec(shape, lambda i: (0,) * len(shape))


def _proj0(x2d, seq, norm_g, w_bf16, gmat, gains, tables):
    n = x2d.shape[0]
    pos_blocks = seq // PROJ_ROWS
    tab_spec = pl.BlockSpec((PROJ_ROWS, LANES), lambda i: (i % pos_blocks, 0))
    widths = (A_Q_W, A_KV_W, A_KV_W, B_QK_W, B_QK_W, B_V_W, EVEN_MIX_W)
    return pl.pallas_call(
        _proj0_kernel,
        grid=(n // PROJ_ROWS,),
        in_specs=[_row_spec(D_MODEL), _const_spec((1, D_MODEL)), _const_spec(w_bf16.shape),
                  _const_spec((NORM_W, NORM_W))] + [_const_spec((1, LANES))] * 4 + [tab_spec] * 6,
        out_specs=[_row_spec(w) for w in widths],
        out_shape=[jax.ShapeDtypeStruct((n, w), BF16) for w in widths],
        compiler_params=pltpu.CompilerParams(dimension_semantics=("parallel",),
                                             vmem_limit_bytes=VMEM_LIMIT),
        name="proj0",
    )(x2d, norm_g, w_bf16, gmat, *gains, *tables)


def _scores(q, k):
    return lax.dot_general(q, k, (((1,), (1,)), ((), ())), preferred_element_type=F32)


def _exp_scores(q, k, small):
    s = _scores(q, k)
    if not small:
        s = s - jnp.max(s, axis=-1, keepdims=True)
    return jnp.exp(s).astype(BF16)


def _attn0_body(lam_init, small, lq1_ref, lk1_ref, lq2_ref, lk2_ref, subln_ref,
                qa_ref, qb_ref, sg_ref, x_ref, ka_ref, va_ref, kb_ref, vb_ref, w_ref, y_ref):
    lam = (jnp.exp(jnp.sum(lq1_ref[...] * lk1_ref[...], axis=-1, keepdims=True))
           - jnp.exp(jnp.sum(lq2_ref[...] * lk2_ref[...], axis=-1, keepdims=True)) + lam_init)
    ones_cols = jnp.ones((ka_ref.shape[0], LANES), BF16)

    outs = []
    group = A_Q_HEADS // A_KV_HEADS
    va_ext = jnp.concatenate([va_ref[...], ones_cols], axis=1)
    for g in range(A_KV_HEADS):
        k = ka_ref[:, g * HEAD_DIM:(g + 1) * HEAD_DIM]
        for j in range(group):
            hq = g * group + j
            e = _exp_scores(qa_ref[:, hq * HEAD_DIM:(hq + 1) * HEAD_DIM], k, small)
            o2 = jnp.dot(e, va_ext, preferred_element_type=F32)
            outs.append(o2[:, g * HEAD_DIM:(g + 1) * HEAD_DIM] * (1.0 / o2[:, LANES:LANES + HEAD_DIM]))

    subln = subln_ref[...]
    for hb in range(B_HEADS):
        c0 = hb * 2 * HEAD_DIM
        maps = [slice(c0 + c * HEAD_DIM, c0 + (c + 1) * HEAD_DIM) for c in range(2)]
        if small:
            ef = [jnp.exp(_scores(qb_ref[:, cols], kb_ref[:, cols])) for cols in maps]
            l0, l1 = [jnp.sum(e, axis=-1, keepdims=True) for e in ef]
            a = ef[0].astype(BF16) - ef[1].astype(BF16) * (lam * l0 / l1).astype(BF16)
            o = jnp.dot(a, vb_ref[:, c0:c0 + 2 * HEAD_DIM], preferred_element_type=F32) * (1.0 / l0)
        else:
            vb_ext = jnp.concatenate([vb_ref[:, c0:c0 + 2 * HEAD_DIM], ones_cols], axis=1)
            parts = [jnp.dot(_exp_scores(qb_ref[:, cols], kb_ref[:, cols], small), vb_ext,
                             preferred_element_type=F32) for cols in maps]
            o = (parts[0][:, :LANES] * (1.0 / parts[0][:, LANES:])
                 - parts[1][:, :LANES] * (lam / parts[1][:, LANES:]))
        outs.append(_rms_rows(o, subln) * (1.0 - lam_init))

    mix = (jnp.concatenate(outs, axis=-1) * sg_ref[...].astype(F32)).astype(BF16)
    y_ref[...] = x_ref[...] + jnp.dot(mix, w_ref[...], preferred_element_type=F32)


def _attn0(small, lam_init, x3d, qa, ka, va, qb, kb, vb, sg, w_out, lams, subln):
    b, s, _ = x3d.shape
    rows = ATTN_SCORE_ELEMS // s
    r3 = lambda t: t.reshape(b, s, t.shape[-1])
    qa, ka, va, qb, kb, vb, sg = map(r3, (qa, ka, va, qb, kb, vb, sg))
    row = lambda w: pl.BlockSpec((None, rows, w), lambda bi, i: (bi, i, 0))
    full = lambda w: pl.BlockSpec((None, s, w), lambda bi, i: (bi, 0, 0))
    const = lambda shape: pl.BlockSpec(shape, lambda bi, i: (0,) * len(shape))
    return pl.pallas_call(
        functools.partial(_attn0_body, lam_init, small),
        grid=(b, s // rows),
        in_specs=[const((1, HEAD_DIM))] * 4 + [const((1, 2 * HEAD_DIM)),
                  row(A_Q_W), row(B_QK_W), row(EVEN_MIX_W), row(D_MODEL),
                  full(A_KV_W), full(A_KV_W), full(B_QK_W), full(B_V_W), const(w_out.shape)],
        out_specs=row(D_MODEL),
        out_shape=jax.ShapeDtypeStruct((b, s, D_MODEL), F32),
        compiler_params=pltpu.CompilerParams(dimension_semantics=("parallel", "arbitrary"),
                                             vmem_limit_bytes=VMEM_LIMIT),
        name="attn0" if small else "attn0_rowmax",
    )(*lams, subln, qa, qb, sg, x3d, ka, va, kb, vb, w_out)


def _proj1_kernel(x_ref, g_ref, w_ref, gmat_ref, cq_ref, ck_ref, prc_ref, prsa_ref, prsb_ref,
                  q_ref, k_ref, v_ref, sg_ref):
    h, row_scale = _scaled_rows(x_ref, g_ref)
    gmat = gmat_ref[...]
    pr = tuple(_wide(t[...]) for t in (prc_ref, prsa_ref, prsb_ref))

    def proj(c0, width):
        return jnp.dot(h, w_ref[:, c0:c0 + width], preferred_element_type=F32) * row_scale

    for out_ref, gain_ref, c0, scale in ((q_ref, cq_ref, 0, QK_SCALE), (k_ref, ck_ref, C_W, 1.0)):
        gain = _wide(gain_ref[...])
        for half_w in range(2):
            z = proj(c0 + half_w * (C_W // 2), C_W // 2)
            for c in range(C_W // 2 // NORM_W):
                sl = slice(c * NORM_W, (c + 1) * NORM_W)
                out_ref[:, half_w * (C_W // 2) + c * NORM_W: half_w * (C_W // 2) + (c + 1) * NORM_W] = (
                    _head_norm_rope(z[:, sl], gmat, gain, *pr, ROPE_DIMS // 2, scale))
    v_ref[...] = proj(2 * C_W, C_W).astype(BF16)
    gate = proj(3 * C_W, C_W)
    sg_ref[...] = (gate * jax.nn.sigmoid(gate)).astype(BF16)


def _proj1(x2d, seq, norm_g, w_bf16, gmat, gains, tables):
    n = x2d.shape[0]
    pos_blocks = seq // PROJ_ROWS
    tab_spec = pl.BlockSpec((PROJ_ROWS, LANES), lambda i: (i % pos_blocks, 0))
    return pl.pallas_call(
        _proj1_kernel,
        grid=(n // PROJ_ROWS,),
        in_specs=[_row_spec(D_MODEL), _const_spec((1, D_MODEL)), _const_spec(w_bf16.shape),
                  _const_spec((NORM_W, NORM_W))] + [_const_spec((1, LANES))] * 2 + [tab_spec] * 3,
        out_specs=[_row_spec(C_W)] * 4,
        out_shape=[jax.ShapeDtypeStruct((n, C_W), BF16)] * 4,
        compiler_params=pltpu.CompilerParams(dimension_semantics=("parallel",),
                                             vmem_limit_bytes=VMEM_LIMIT),
        name="proj1",
    )(x2d, norm_g, w_bf16, gmat, *gains, *tables)


def _dilated_kernel(seq, small, q_ref, k_ref, v_ref, o_ref, qf, kf, vf, qd, kd, vd, bias, acc, den, mx=None):
    qf[...] = q_ref[...].astype(F32)
    kf[...] = k_ref[...].astype(F32)
    vf[...] = v_ref[...].astype(F32)

    rr = lax.broadcasted_iota(jnp.int32, (BAND_ROWS, 2 * BAND_ROWS), 0)
    cc = lax.broadcasted_iota(jnp.int32, (BAND_ROWS, 2 * BAND_ROWS), 1)
    band = (cc >= rr) & (cc <= rr + 2 * BAND_RADIUS)
    not_before = cc >= BAND_RADIUS
    not_after = cc < BAND_ROWS + BAND_RADIUS
    for idx, ok in enumerate((band, band & not_before, band & not_after, band & not_before & not_after)):
        bias[idx] = jnp.where(ok, 0.0, NEG).astype(F32)

    lane = lax.broadcasted_iota(jnp.int32, (1, LANES), 1)
    head0 = lane < HEAD_DIM
    zeros_pad = jnp.zeros((BAND_RADIUS, LANES), BF16)
    ones_cols = jnp.ones((2 * BAND_ROWS, LANES), BF16)

    for p, (_, dil) in enumerate(C_PATTERNS):
        length = seq // dil
        padded = length + 2 * BAND_RADIUS
        blocks = length // BAND_ROWS
        q_src = q_ref if dil == 1 else qd
        for r in range(dil):
            if dil == 1:
                k_rows, v_rows = k_ref[...], v_ref[...]
            else:
                rows = pl.ds(r, length, stride=dil)
                qd[r * length:(r + 1) * length, :] = qf[rows, :].astype(BF16)
                k_rows, v_rows = kf[rows, :].astype(BF16), vf[rows, :].astype(BF16)
            kd[r * padded:r * padded + BAND_RADIUS, :] = zeros_pad
            kd[r * padded + BAND_RADIUS:r * padded + BAND_RADIUS + length, :] = k_rows
            kd[r * padded + BAND_RADIUS + length:(r + 1) * padded, :] = zeros_pad
            vd[r * padded:r * padded + BAND_RADIUS, :] = zeros_pad
            vd[r * padded + BAND_RADIUS:r * padded + BAND_RADIUS + length, :] = v_rows
            vd[r * padded + BAND_RADIUS + length:(r + 1) * padded, :] = zeros_pad

        def block(t, carry, dil=dil, length=length, padded=padded, blocks=blocks, p=p, q_src=q_src):
            r = t // blocks
            i = t - r * blocks
            q0 = pl.multiple_of(r * length + i * BAND_ROWS, BAND_ROWS)
            k0 = pl.multiple_of(r * padded + i * BAND_ROWS, BAND_RADIUS)
            qb = q_src[pl.ds(q0, BAND_ROWS), :]
            kb = kd[pl.ds(k0, 2 * BAND_ROWS), :]
            vb = vd[pl.ds(k0, 2 * BAND_ROWS), :]
            which = jnp.where(i == 0, 1, 0) + jnp.where(i == blocks - 1, 2, 0)
            bias_blk = bias[which]
            zq = jnp.zeros_like(qb)
            qs = jnp.concatenate([jnp.where(head0, qb, zq), jnp.where(head0, zq, qb)], axis=0)
            s = lax.dot_general(qs, kb, (((1,), (1,)), ((), ())), preferred_element_type=F32)
            s = s + jnp.concatenate([bias_blk, bias_blk], axis=0)
            if small:
                e = jnp.exp(s).astype(BF16)
            else:
                m = jnp.max(s, axis=-1, keepdims=True)
                e = jnp.exp(s - m).astype(BF16)
            o2 = jnp.dot(e, jnp.concatenate([vb, ones_cols], axis=1), preferred_element_type=F32)
            o = jnp.where(head0, o2[:BAND_ROWS, :LANES], o2[BAND_ROWS:, :LANES])
            l2 = jnp.where(head0, o2[:BAND_ROWS, LANES:], o2[BAND_ROWS:, LANES:])
            if dil > 1:
                dst = pl.ds(i * (BAND_ROWS * dil) + r, BAND_ROWS, stride=dil)
            else:
                dst = pl.ds(q0, BAND_ROWS)
            acc[p, dst, :] = o
            den[p, dst, :] = l2
            if not small:
                mb = jnp.broadcast_to(m, (2 * BAND_ROWS, LANES))
                mx[p, dst, :] = jnp.where(head0, mb[:BAND_ROWS], mb[BAND_ROWS:])
            return carry

        lax.fori_loop(0, dil * blocks, block, 0, unroll=BLOCK_UNROLL_SMALL if small else BLOCK_UNROLL)

    num = jnp.zeros((seq, LANES), F32)
    dsum = jnp.zeros((seq, LANES), F32)
    if small:
        for p in range(len(C_PATTERNS)):
            num = num + acc[p]
            dsum = dsum + den[p]
    else:
        m_all = jnp.maximum(jnp.maximum(mx[0], mx[1]), mx[2])
        for p in range(len(C_PATTERNS)):
            w = jnp.exp(mx[p] - m_all)
            num = num + w * acc[p]
            dsum = dsum + w * den[p]
    o_ref[...] = (num / dsum).astype(BF16)


def _dilated(small, q, k, v, b, s):
    r3 = lambda t: t.reshape(b, s, C_W)
    spec = pl.BlockSpec((None, s, LANES), lambda bi, hp: (bi, 0, hp))
    n_pat = len(C_PATTERNS)
    max_dil = max(d for _, d in C_PATTERNS)
    padded_rows = s + 2 * BAND_RADIUS * max_dil
    return pl.pallas_call(
        functools.partial(_dilated_kernel, s, small),
        grid=(b, C_W // LANES),
        in_specs=[spec] * 3,
        out_specs=spec,
        out_shape=jax.ShapeDtypeStruct((b, s, C_W), BF16),
        scratch_shapes=[pltpu.VMEM((s, LANES), F32)] * 3
        + [pltpu.VMEM((s, LANES), BF16)]
        + [pltpu.VMEM((padded_rows, LANES), BF16)] * 2
        + [pltpu.VMEM((4, BAND_ROWS, 2 * BAND_ROWS), F32)]
        + [pltpu.VMEM((n_pat, s, LANES), F32)] * (2 if small else 3),
        compiler_params=pltpu.CompilerParams(dimension_semantics=("parallel", "arbitrary"),
                                             vmem_limit_bytes=VMEM_LIMIT),
        name="dilated" if small else "dilated_rowmax",
    )(r3(q), r3(k), r3(v))


def _out1_kernel(o_ref, sg_ref, x_ref, w_ref, y_ref):
    mix = (o_ref[...].astype(F32) * sg_ref[...].astype(F32)).astype(BF16)
    y_ref[...] = x_ref[...] + jnp.dot(mix, w_ref[...], preferred_element_type=F32)


def _out1(o2d, sg, x2d, w_bf16):
    n = x2d.shape[0]
    return pl.pallas_call(
        _out1_kernel,
        grid=(n // PROJ_ROWS,),
        in_specs=[_row_spec(C_W), _row_spec(C_W), _row_spec(D_MODEL), _const_spec(w_bf16.shape)],
        out_specs=_row_spec(D_MODEL),
        out_shape=jax.ShapeDtypeStruct((n, D_MODEL), F32),
        compiler_params=pltpu.CompilerParams(dimension_semantics=("parallel",),
                                             vmem_limit_bytes=VMEM_LIMIT),
        name="out1",
    )(o2d, sg, x2d, w_bf16)


def _trunk(x, p):
    b, s, _ = x.shape
    x2d = x.reshape(b * s, D_MODEL)
    qa, ka, va, qb, kb, vb, sg = _proj0(x2d, s, p["norm0"], p["w_in0"], p["gmat"], p["gains0"], p["tables"])
    y0 = lax.cond(p["small0"][0] > 0,
                  functools.partial(_attn0, True, p["lam_init"]), functools.partial(_attn0, False, p["lam_init"]),
                  x, qa, ka, va, qb, kb, vb, sg, p["w_out0"], p["lams"], p["subln"])
    y0_2d = y0.reshape(b * s, D_MODEL)
    q, k, v, sg1 = _proj1(y0_2d, s, p["norm1"], p["w_in1"], p["gmat"], p["gains1"], p["tables"][3:])
    o = lax.cond(p["small1"][0] > 0, functools.partial(_dilated, True, b=b, s=s),
                 functools.partial(_dilated, False, b=b, s=s), q, k, v)
    y1 = _out1(o.reshape(b * s, C_W), sg1, y0_2d, p["w_out1"])
    return y1.reshape(b, s, D_MODEL)


def kernel(x_prompt, x_sample, norm0, w_in0, w_out0, a_q_norm, a_k_norm, b_q_norm, b_k_norm, lambda_q1, lambda_k1, lambda_q2, lambda_k2, b_subln, norm1, w_in1, w_out1, c_q_norm, c_k_norm):
    two = lambda g: jnp.concatenate([g, g]).reshape(1, LANES).astype(F32)
    max_seq = max(x_prompt.shape[1], x_sample.shape[1])
    params = {
        "norm0": norm0.reshape(1, D_MODEL), "norm1": norm1.reshape(1, D_MODEL),
        "w_in0": w_in0.astype(BF16), "w_out0": w_out0.astype(BF16),
        "w_in1": w_in1.astype(BF16), "w_out1": w_out1.astype(BF16),
        "gmat": _head_mean_matrix(),
        "gains0": tuple(two(g) for g in (a_q_norm, a_k_norm, b_q_norm, b_k_norm)),
        "gains1": tuple(two(g) for g in (c_q_norm, c_k_norm)),
        "tables": _rope_tables(max_seq),
        "lams": tuple(t.reshape(1, HEAD_DIM) for t in (lambda_q1, lambda_k1, lambda_q2, lambda_k2)),
        "subln": b_subln.reshape(1, 2 * HEAD_DIM),
        "lam_init": 0.8 - 0.6 * math.exp(-0.3 * 0),
        "small0": _scores_are_small(a_q_norm, a_k_norm) * _scores_are_small(b_q_norm, b_k_norm),
        "small1": _scores_are_small(c_q_norm, c_k_norm),
    }
    return (_trunk(x_prompt, params), _trunk(x_sample, params))
```

```python
import functools
import math

import jax
import jax.numpy as jnp
from jax import lax
from jax.experimental import pallas as pl
from jax.experimental.pallas import tpu as pltpu

D_MODEL = 1024
HEAD_DIM = 64
GRID_W = 64
EPS = 1e-6
NEG = -1e30

A_Q_HEADS = 8
A_KV_HEADS = 2
AXIAL_THETA = 10000.0
B_HEADS = 4
C_HEADS = 16
C_PATTERNS = ((128, 1), (512, 4), (2048, 16))
ROPE_THETA = 500000.0
ROPE_DIMS = HEAD_DIM // 4

A_Q_W = A_Q_HEADS * HEAD_DIM
A_KV_W = A_KV_HEADS * HEAD_DIM
B_QK_W = B_HEADS * 2 * HEAD_DIM
B_V_W = B_HEADS * 2 * HEAD_DIM
EVEN_MIX_W = A_Q_W + B_V_W
C_W = C_HEADS * HEAD_DIM

LANES = 128
NORM_W = 256
VMEM_LIMIT = 56 * 1024 * 1024
PROJ_ROWS = 1024
ATTN_SCORE_ELEMS = 512 * 2048
ATTN_ROWS_ROWMAX = 256
B_HEAD_GROUP = 2
BAND_ROWS = 128
BAND_RADIUS = 64
BLOCK_UNROLL = 4
BLOCK_UNROLL_SMALL = 32
SAFE_SCORE = 30.0
NORM_SLACK = 1.02
QK_SCALE = HEAD_DIM ** -0.5

BF16 = jnp.bfloat16
F32 = jnp.float32


def _rope_tables(n_pos):
    pos = jnp.arange(n_pos, dtype=F32)
    lane = jnp.arange(HEAD_DIM)

    def angles(p, n_dims, theta):
        freqs = theta ** (-jnp.arange(0, n_dims, 2, dtype=F32) / n_dims)
        ang = p[:, None] * freqs[None, :]
        return jnp.concatenate([ang, ang], axis=-1)

    half = HEAD_DIM // 2
    row = jnp.floor(pos / GRID_W)
    col = pos - row * GRID_W
    ang = jnp.concatenate([angles(row, half, AXIAL_THETA), angles(col, half, AXIAL_THETA)], axis=-1)
    first = (lane % half) < (half // 2)
    ax_c = jnp.cos(ang)
    ax_sa = jnp.where(first[None, :], -jnp.sin(ang), 0.0)
    ax_sb = jnp.where(first[None, :], 0.0, jnp.sin(ang))

    angp = angles(pos, ROPE_DIMS, ROPE_THETA)
    pad = jnp.zeros((n_pos, HEAD_DIM - ROPE_DIMS), F32)
    cosp = jnp.concatenate([jnp.cos(angp), pad + 1.0], axis=-1)
    sinp = jnp.concatenate([jnp.sin(angp), pad], axis=-1)
    p_first = lane < (ROPE_DIMS // 2)
    p_second = (lane >= ROPE_DIMS // 2) & (lane < ROPE_DIMS)
    pr_c = cosp
    pr_sa = jnp.where(p_first[None, :], -sinp, 0.0)
    pr_sb = jnp.where(p_second[None, :], sinp, 0.0)

    two = lambda t: jnp.concatenate([t, t], axis=-1).astype(F32)
    return tuple(two(t) for t in (ax_c, ax_sa, ax_sb, pr_c, pr_sa, pr_sb))


def _scores_are_small(q_gain, k_gain):
    bound = (HEAD_DIM ** 0.5) * NORM_SLACK * jnp.max(jnp.abs(q_gain)) * jnp.max(jnp.abs(k_gain))
    return (bound <= SAFE_SCORE).astype(jnp.int32).reshape(1)


def _head_mean_matrix():
    i = jnp.arange(NORM_W)
    same = (i[:, None] // HEAD_DIM) == (i[None, :] // HEAD_DIM)
    return jnp.where(same, 1.0 / HEAD_DIM, 0.0).astype(BF16)


def _rms_rows(x, g):
    return x * lax.rsqrt(jnp.mean(x * x, axis=-1, keepdims=True) + EPS) * g


def _scaled_rows(x_ref, g_ref):
    x = x_ref[...]
    return (x * g_ref[...]).astype(BF16), lax.rsqrt(jnp.mean(x * x, axis=-1, keepdims=True) + EPS)


def _wide(t):
    return jnp.concatenate([t, t], axis=1)


def _head_norm_rope(x, gmat, gain, c, sa, sb, half, scale):
    ss = jnp.dot((x * x).astype(BF16), gmat, preferred_element_type=F32)
    y = x * lax.rsqrt(ss + EPS) * gain
    out = y * c + pltpu.roll(y, NORM_W - half, 1) * sa + pltpu.roll(y, half, 1) * sb
    if scale != 1.0:
        out = out * scale
    return out.astype(BF16)


def _proj0_kernel(x_ref, g_ref, w_ref, gmat_ref, aq_ref, ak_ref, bq_ref, bk_ref,
                  axc_ref, axsa_ref, axsb_ref, prc_ref, prsa_ref, prsb_ref,
                  qa_ref, ka_ref, va_ref, qb_ref, kb_ref, vb_ref, sg_ref, vat_ref, vbt_ref):
    h, row_scale = _scaled_rows(x_ref, g_ref)
    gmat = gmat_ref[...]
    ax = tuple(_wide(t[...]) for t in (axc_ref, axsa_ref, axsb_ref))
    pr = tuple(_wide(t[...]) for t in (prc_ref, prsa_ref, prsb_ref))

    def proj(c0, width):
        return jnp.dot(h, w_ref[:, c0:c0 + width], preferred_element_type=F32) * row_scale

    def normed(z, out_ref, gain, tabs, half, scale):
        for c in range(z.shape[1] // NORM_W):
            sl = slice(c * NORM_W, (c + 1) * NORM_W)
            out_ref[:, sl] = _head_norm_rope(z[:, sl], gmat, _wide(gain), *tabs, half, scale)

    c0 = 0
    normed(proj(c0, A_Q_W), qa_ref, aq_ref[...], ax, HEAD_DIM // 4, QK_SCALE)
    c0 += A_Q_W
    zkv = proj(c0, 2 * A_KV_W)
    kv = _head_norm_rope(zkv, gmat, _wide(ak_ref[...]), *ax, HEAD_DIM // 4, 1.0)
    ka_ref[...] = kv[:, :A_KV_W]
    zv = zkv[:, A_KV_W:]
    va_ref[...] = zv.astype(BF16)
    zv_t = zv.T
    ones_rows = jnp.ones((HEAD_DIM, zv_t.shape[1]), F32)
    for g in range(A_KV_HEADS):
        vat_ref[g] = jnp.concatenate([zv_t[g * HEAD_DIM:(g + 1) * HEAD_DIM], ones_rows], axis=0).astype(BF16)
    c0 += 2 * A_KV_W
    normed(proj(c0, B_QK_W), qb_ref, bq_ref[...], pr, ROPE_DIMS // 2, QK_SCALE)
    c0 += B_QK_W
    normed(proj(c0, B_QK_W), kb_ref, bk_ref[...], pr, ROPE_DIMS // 2, 1.0)
    c0 += B_QK_W
    zb = proj(c0, B_V_W)
    vb_ref[...] = zb.astype(BF16)
    zb_t = zb.T
    for hb in range(B_HEADS):
        vbt_ref[hb] = zb_t[hb * 2 * HEAD_DIM:(hb + 1) * 2 * HEAD_DIM].astype(BF16)
    c0 += B_V_W
    gate = proj(c0, EVEN_MIX_W)
    sg_ref[...] = (gate * jax.nn.sigmoid(gate)).astype(BF16)


def _row_spec(width, rows=PROJ_ROWS):
    return pl.BlockSpec((rows, width), lambda i: (i, 0))


def _const_spec(shape):
    return pl.BlockSpec(shape, lambda i: (0,) * len(shape))


def _proj0(x2d, seq, norm_g, w_bf16, gmat, gains, tables):
    n = x2d.shape[0]
    pos_blocks = seq // PROJ_ROWS
    tab_spec = pl.BlockSpec((PROJ_ROWS, LANES), lambda i: (i % pos_blocks, 0))
    widths = (A_Q_W, A_KV_W, A_KV_W, B_QK_W, B_QK_W, B_V_W, EVEN_MIX_W)
    t_spec = lambda heads: pl.BlockSpec((None, heads, LANES, PROJ_ROWS),
                                        lambda i: (i // pos_blocks, 0, 0, i % pos_blocks))
    t_shape = lambda heads: jax.ShapeDtypeStruct((n // seq, heads, LANES, seq), BF16)
    return pl.pallas_call(
        _proj0_kernel,
        grid=(n // PROJ_ROWS,),
        in_specs=[_row_spec(D_MODEL), _const_spec((1, D_MODEL)), _const_spec(w_bf16.shape),
                  _const_spec((NORM_W, NORM_W))] + [_const_spec((1, LANES))] * 4 + [tab_spec] * 6,
        out_specs=[_row_spec(w) for w in widths] + [t_spec(A_KV_HEADS), t_spec(B_HEADS)],
        out_shape=[jax.ShapeDtypeStruct((n, w), BF16) for w in widths] + [t_shape(A_KV_HEADS), t_shape(B_HEADS)],
        compiler_params=pltpu.CompilerParams(dimension_semantics=("parallel",),
                                             vmem_limit_bytes=VMEM_LIMIT),
        name="proj0",
    )(x2d, norm_g, w_bf16, gmat, *gains, *tables)


def _scores(q, k):
    return lax.dot_general(q, k, (((1,), (1,)), ((), ())), preferred_element_type=F32)


def _lambda(lam_init, lq1_ref, lk1_ref, lq2_ref, lk2_ref):
    return (jnp.exp(jnp.sum(lq1_ref[...] * lk1_ref[...], axis=-1, keepdims=True))
            - jnp.exp(jnp.sum(lq2_ref[...] * lk2_ref[...], axis=-1, keepdims=True)) + lam_init)


def _attn0_small_body(lam_init, lq1_ref, lk1_ref, lq2_ref, lk2_ref, subln_ref,
                      qa_ref, qb_ref, sg_ref, x_ref, ka_ref, vat_ref, kb_ref, vbt_ref, w_ref, y_ref):
    lam = _lambda(lam_init, lq1_ref, lk1_ref, lq2_ref, lk2_ref)

    def exp_scores_t(k, q):
        return jnp.exp(_scores(k, q))

    outs_t = []
    group = A_Q_HEADS // A_KV_HEADS
    for g in range(A_KV_HEADS):
        k = ka_ref[:, g * HEAD_DIM:(g + 1) * HEAD_DIM]
        vt = vat_ref[g]
        es = [exp_scores_t(k, qa_ref[:, (g * group + j) * HEAD_DIM:(g * group + j + 1) * HEAD_DIM]).astype(BF16)
              for j in range(group)]
        for e in es:
            o2 = jnp.dot(vt, e, preferred_element_type=F32)
            outs_t.append(o2[:HEAD_DIM] * (1.0 / o2[HEAD_DIM:]))

    for hb0 in range(0, B_HEADS, B_HEAD_GROUP):
        combined = []
        for hb in range(hb0, hb0 + B_HEAD_GROUP):
            c0 = hb * 2 * HEAD_DIM
            ef = [exp_scores_t(kb_ref[:, c0 + c * HEAD_DIM:c0 + (c + 1) * HEAD_DIM],
                               qb_ref[:, c0 + c * HEAD_DIM:c0 + (c + 1) * HEAD_DIM]) for c in range(2)]
            l0, l1 = [jnp.sum(e, axis=0, keepdims=True) for e in ef]
            a = ef[0].astype(BF16) - ef[1].astype(BF16) * (lam * l0 / l1).astype(BF16)
            combined.append((a, l0))
        for hb, (a, l0) in zip(range(hb0, hb0 + B_HEAD_GROUP), combined):
            outs_t.append(jnp.dot(vbt_ref[hb], a, preferred_element_type=F32) * (1.0 / l0))

    o = jnp.concatenate(outs_t, axis=0).T
    subln = subln_ref[...]
    parts = [o[:, :A_Q_W]]
    for hb in range(B_HEADS):
        c0 = A_Q_W + hb * 2 * HEAD_DIM
        parts.append(_rms_rows(o[:, c0:c0 + 2 * HEAD_DIM], subln) * (1.0 - lam_init))
    mix = (jnp.concatenate(parts, axis=-1) * sg_ref[...].astype(F32)).astype(BF16)
    y_ref[...] = x_ref[...] + jnp.dot(mix, w_ref[...], preferred_element_type=F32)


def _attn0_rowmax_body(lam_init, lq1_ref, lk1_ref, lq2_ref, lk2_ref, subln_ref,
                       qa_ref, qb_ref, sg_ref, x_ref, ka_ref, va_ref, kb_ref, vb_ref, w_ref, y_ref):
    lam = _lambda(lam_init, lq1_ref, lk1_ref, lq2_ref, lk2_ref)
    ones_cols = jnp.ones((ka_ref.shape[0], LANES), BF16)

    def exp_scores(q, k):
        s = _scores(q, k)
        return jnp.exp(s - jnp.max(s, axis=-1, keepdims=True)).astype(BF16)

    outs = []
    group = A_Q_HEADS // A_KV_HEADS
    va_ext = jnp.concatenate([va_ref[...], ones_cols], axis=1)
    for g in range(A_KV_HEADS):
        k = ka_ref[:, g * HEAD_DIM:(g + 1) * HEAD_DIM]
        for j in range(group):
            hq = g * group + j
            o2 = jnp.dot(exp_scores(qa_ref[:, hq * HEAD_DIM:(hq + 1) * HEAD_DIM], k), va_ext,
                         preferred_element_type=F32)
            outs.append(o2[:, g * HEAD_DIM:(g + 1) * HEAD_DIM] * (1.0 / o2[:, LANES:LANES + HEAD_DIM]))

    subln = subln_ref[...]
    for hb in range(B_HEADS):
        c0 = hb * 2 * HEAD_DIM
        vb_ext = jnp.concatenate([vb_ref[:, c0:c0 + 2 * HEAD_DIM], ones_cols], axis=1)
        parts = [jnp.dot(exp_scores(qb_ref[:, c0 + c * HEAD_DIM:c0 + (c + 1) * HEAD_DIM],
                                    kb_ref[:, c0 + c * HEAD_DIM:c0 + (c + 1) * HEAD_DIM]), vb_ext,
                         preferred_element_type=F32) for c in range(2)]
        o = (parts[0][:, :LANES] * (1.0 / parts[0][:, LANES:])
             - parts[1][:, :LANES] * (lam / parts[1][:, LANES:]))
        outs.append(_rms_rows(o, subln) * (1.0 - lam_init))

    mix = (jnp.concatenate(outs, axis=-1) * sg_ref[...].astype(F32)).astype(BF16)
    y_ref[...] = x_ref[...] + jnp.dot(mix, w_ref[...], preferred_element_type=F32)


def _attn0(small, lam_init, x3d, qa, ka, qb, kb, sg, w_out, lams, subln, v_a, v_b):
    b, s, _ = x3d.shape
    rows = ATTN_SCORE_ELEMS // s if small else ATTN_ROWS_ROWMAX
    r3 = lambda t: t.reshape(b, s, t.shape[-1])
    qa, ka, qb, kb, sg = map(r3, (qa, ka, qb, kb, sg))
    row = lambda w: pl.BlockSpec((None, rows, w), lambda bi, i: (bi, i, 0))
    full = lambda w: pl.BlockSpec((None, s, w), lambda bi, i: (bi, 0, 0))
    full_t = lambda heads: pl.BlockSpec((None, heads, LANES, s), lambda bi, i: (bi, 0, 0, 0))
    const = lambda shape: pl.BlockSpec(shape, lambda bi, i: (0,) * len(shape))
    if small:
        body, v_specs = _attn0_small_body, (full_t(A_KV_HEADS), full_t(B_HEADS))
    else:
        body, v_specs = _attn0_rowmax_body, (full(A_KV_W), full(B_V_W))
        v_a, v_b = r3(v_a), r3(v_b)
    return pl.pallas_call(
        functools.partial(body, lam_init),
        grid=(b, s // rows),
        in_specs=[const((1, HEAD_DIM))] * 4 + [const((1, 2 * HEAD_DIM)),
                  row(A_Q_W), row(B_QK_W), row(EVEN_MIX_W), row(D_MODEL),
                  full(A_KV_W), v_specs[0], full(B_QK_W), v_specs[1], const(w_out.shape)],
        out_specs=row(D_MODEL),
        out_shape=jax.ShapeDtypeStruct((b, s, D_MODEL), F32),
        compiler_params=pltpu.CompilerParams(dimension_semantics=("parallel", "arbitrary"),
                                             vmem_limit_bytes=VMEM_LIMIT),
        name="attn0" if small else "attn0_rowmax",
    )(*lams, subln, qa, qb, sg, x3d, ka, v_a, kb, v_b, w_out)


def _proj1_kernel(x_ref, g_ref, w_ref, gmat_ref, cq_ref, ck_ref, prc_ref, prsa_ref, prsb_ref,
                  q_ref, k_ref, v_ref, sg_ref):
    h, row_scale = _scaled_rows(x_ref, g_ref)
    gmat = gmat_ref[...]
    pr = tuple(_wide(t[...]) for t in (prc_ref, prsa_ref, prsb_ref))

    def proj(c0, width):
        return jnp.dot(h, w_ref[:, c0:c0 + width], preferred_element_type=F32) * row_scale

    for out_ref, gain_ref, c0, scale in ((q_ref, cq_ref, 0, QK_SCALE), (k_ref, ck_ref, C_W, 1.0)):
        gain = _wide(gain_ref[...])
        for half_w in range(2):
            z = proj(c0 + half_w * (C_W // 2), C_W // 2)
            for c in range(C_W // 2 // NORM_W):
                sl = slice(c * NORM_W, (c + 1) * NORM_W)
                out_ref[:, half_w * (C_W // 2) + c * NORM_W: half_w * (C_W // 2) + (c + 1) * NORM_W] = (
                    _head_norm_rope(z[:, sl], gmat, gain, *pr, ROPE_DIMS // 2, scale))
    v_ref[...] = proj(2 * C_W, C_W).astype(BF16)
    gate = proj(3 * C_W, C_W)
    sg_ref[...] = (gate * jax.nn.sigmoid(gate)).astype(BF16)


def _proj1(x2d, seq, norm_g, w_bf16, gmat, gains, tables):
    n = x2d.shape[0]
    pos_blocks = seq // PROJ_ROWS
    tab_spec = pl.BlockSpec((PROJ_ROWS, LANES), lambda i: (i % pos_blocks, 0))
    return pl.pallas_call(
        _proj1_kernel,
        grid=(n // PROJ_ROWS,),
        in_specs=[_row_spec(D_MODEL), _const_spec((1, D_MODEL)), _const_spec(w_bf16.shape),
                  _const_spec((NORM_W, NORM_W))] + [_const_spec((1, LANES))] * 2 + [tab_spec] * 3,
        out_specs=[_row_spec(C_W)] * 4,
        out_shape=[jax.ShapeDtypeStruct((n, C_W), BF16)] * 4,
        compiler_params=pltpu.CompilerParams(dimension_semantics=("parallel",),
                                             vmem_limit_bytes=VMEM_LIMIT),
        name="proj1",
    )(x2d, norm_g, w_bf16, gmat, *gains, *tables)


def _dilated_kernel(seq, small, q_ref, k_ref, v_ref, o_ref, qf, kf, vf, qd, kd, vd, bias, acc, den, mx=None):
    qf[...] = q_ref[...].astype(F32)
    kf[...] = k_ref[...].astype(F32)
    vf[...] = v_ref[...].astype(F32)

    rr = lax.broadcasted_iota(jnp.int32, (BAND_ROWS, 2 * BAND_ROWS), 0)
    cc = lax.broadcasted_iota(jnp.int32, (BAND_ROWS, 2 * BAND_ROWS), 1)
    band = (cc >= rr) & (cc <= rr + 2 * BAND_RADIUS)
    not_before = cc >= BAND_RADIUS
    not_after = cc < BAND_ROWS + BAND_RADIUS
    for idx, ok in enumerate((band, band & not_before, band & not_after, band & not_before & not_after)):
        bias[idx] = jnp.where(ok, 0.0, NEG).astype(F32)

    lane = lax.broadcasted_iota(jnp.int32, (1, LANES), 1)
    head0 = lane < HEAD_DIM
    zeros_pad = jnp.zeros((BAND_RADIUS, LANES), BF16)
    ones_cols = jnp.ones((2 * BAND_ROWS, LANES), BF16)

    for p, (_, dil) in enumerate(C_PATTERNS):
        length = seq // dil
        padded = length + 2 * BAND_RADIUS
        blocks = length // BAND_ROWS
        q_src = q_ref if dil == 1 else qd
        for r in range(dil):
            if dil == 1:
                k_rows, v_rows = k_ref[...], v_ref[...]
            else:
                rows = pl.ds(r, length, stride=dil)
                qd[r * length:(r + 1) * length, :] = qf[rows, :].astype(BF16)
                k_rows, v_rows = kf[rows, :].astype(BF16), vf[rows, :].astype(BF16)
            kd[r * padded:r * padded + BAND_RADIUS, :] = zeros_pad
            kd[r * padded + BAND_RADIUS:r * padded + BAND_RADIUS + length, :] = k_rows
            kd[r * padded + BAND_RADIUS + length:(r + 1) * padded, :] = zeros_pad
            vd[r * padded:r * padded + BAND_RADIUS, :] = zeros_pad
            vd[r * padded + BAND_RADIUS:r * padded + BAND_RADIUS + length, :] = v_rows
            vd[r * padded + BAND_RADIUS + length:(r + 1) * padded, :] = zeros_pad

        def block(t, carry, dil=dil, length=length, padded=padded, blocks=blocks, p=p, q_src=q_src):
            r = t // blocks
            i = t - r * blocks
            q0 = pl.multiple_of(r * length + i * BAND_ROWS, BAND_ROWS)
            k0 = pl.multiple_of(r * padded + i * BAND_ROWS, BAND_RADIUS)
            qb = q_src[pl.ds(q0, BAND_ROWS), :]
            kb = kd[pl.ds(k0, 2 * BAND_ROWS), :]
            vb = vd[pl.ds(k0, 2 * BAND_ROWS), :]
            which = jnp.where(i == 0, 1, 0) + jnp.where(i == blocks - 1, 2, 0)
            bias_blk = bias[which]
            zq = jnp.zeros_like(qb)
            qs = jnp.concatenate([jnp.where(head0, qb, zq), jnp.where(head0, zq, qb)], axis=0)
            s = lax.dot_general(qs, kb, (((1,), (1,)), ((), ())), preferred_element_type=F32)
            s = s + jnp.concatenate([bias_blk, bias_blk], axis=0)
            if small:
                e = jnp.exp(s).astype(BF16)
            else:
                m = jnp.max(s, axis=-1, keepdims=True)
                e = jnp.exp(s - m).astype(BF16)
            o2 = jnp.dot(e, jnp.concatenate([vb, ones_cols], axis=1), preferred_element_type=F32)
            o = jnp.where(head0, o2[:BAND_ROWS, :LANES], o2[BAND_ROWS:, :LANES])
            l2 = jnp.where(head0, o2[:BAND_ROWS, LANES:], o2[BAND_ROWS:, LANES:])
            if dil > 1:
                dst = pl.ds(i * (BAND_ROWS * dil) + r, BAND_ROWS, stride=dil)
            else:
                dst = pl.ds(q0, BAND_ROWS)
            acc[p, dst, :] = o
            den[p, dst, :] = l2
            if not small:
                mb = jnp.broadcast_to(m, (2 * BAND_ROWS, LANES))
                mx[p, dst, :] = jnp.where(head0, mb[:BAND_ROWS], mb[BAND_ROWS:])
            return carry

        lax.fori_loop(0, dil * blocks, block, 0, unroll=BLOCK_UNROLL_SMALL if small else BLOCK_UNROLL)

    num = jnp.zeros((seq, LANES), F32)
    dsum = jnp.zeros((seq, LANES), F32)
    if small:
        for p in range(len(C_PATTERNS)):
            num = num + acc[p]
            dsum = dsum + den[p]
    else:
        m_all = jnp.maximum(jnp.maximum(mx[0], mx[1]), mx[2])
        for p in range(len(C_PATTERNS)):
            w = jnp.exp(mx[p] - m_all)
            num = num + w * acc[p]
            dsum = dsum + w * den[p]
    o_ref[...] = (num / dsum).astype(BF16)


def _dilated(small, q, k, v, b, s):
    r3 = lambda t: t.reshape(b, s, C_W)
    spec = pl.BlockSpec((None, s, LANES), lambda bi, hp: (bi, 0, hp))
    n_pat = len(C_PATTERNS)
    max_dil = max(d for _, d in C_PATTERNS)
    padded_rows = s + 2 * BAND_RADIUS * max_dil
    return pl.pallas_call(
        functools.partial(_dilated_kernel, s, small),
        grid=(b, C_W // LANES),
        in_specs=[spec] * 3,
        out_specs=spec,
        out_shape=jax.ShapeDtypeStruct((b, s, C_W), BF16),
        scratch_shapes=[pltpu.VMEM((s, LANES), F32)] * 3
        + [pltpu.VMEM((s, LANES), BF16)]
        + [pltpu.VMEM((padded_rows, LANES), BF16)] * 2
        + [pltpu.VMEM((4, BAND_ROWS, 2 * BAND_ROWS), F32)]
        + [pltpu.VMEM((n_pat, s, LANES), F32)] * (2 if small else 3),
        compiler_params=pltpu.CompilerParams(dimension_semantics=("parallel", "arbitrary"),
                                             vmem_limit_bytes=VMEM_LIMIT),
        name="dilated" if small else "dilated_rowmax",
    )(r3(q), r3(k), r3(v))


def _out1_kernel(o_ref, sg_ref, x_ref, w_ref, y_ref):
    mix = (o_ref[...].astype(F32) * sg_ref[...].astype(F32)).astype(BF16)
    y_ref[...] = x_ref[...] + jnp.dot(mix, w_ref[...], preferred_element_type=F32)


def _out1(o2d, sg, x2d, w_bf16):
    n = x2d.shape[0]
    return pl.pallas_call(
        _out1_kernel,
        grid=(n // PROJ_ROWS,),
        in_specs=[_row_spec(C_W), _row_spec(C_W), _row_spec(D_MODEL), _const_spec(w_bf16.shape)],
        out_specs=_row_spec(D_MODEL),
        out_shape=jax.ShapeDtypeStruct((n, D_MODEL), F32),
        compiler_params=pltpu.CompilerParams(dimension_semantics=("parallel",),
                                             vmem_limit_bytes=VMEM_LIMIT),
        name="out1",
    )(o2d, sg, x2d, w_bf16)


def _trunk(x, p):
    b, s, _ = x.shape
    x2d = x.reshape(b * s, D_MODEL)
    qa, ka, va, qb, kb, vb, sg, vat, vbt = _proj0(x2d, s, p["norm0"], p["w_in0"], p["gmat"], p["gains0"],
                                                  p["tables"])
    attn0 = lambda small: (lambda *ops: _attn0(small, p["lam_init"], *ops[:-4],
                                               *(ops[-2:] if small else ops[-4:-2])))
    y0 = lax.cond(p["small0"][0] > 0, attn0(True), attn0(False),
                  x, qa, ka, qb, kb, sg, p["w_out0"], p["lams"], p["subln"], va, vb, vat, vbt)
    y0_2d = y0.reshape(b * s, D_MODEL)
    q, k, v, sg1 = _proj1(y0_2d, s, p["norm1"], p["w_in1"], p["gmat"], p["gains1"], p["tables"][3:])
    o = lax.cond(p["small1"][0] > 0, functools.partial(_dilated, True, b=b, s=s),
                 functools.partial(_dilated, False, b=b, s=s), q, k, v)
    y1 = _out1(o.reshape(b * s, C_W), sg1, y0_2d, p["w_out1"])
    return y1.reshape(b, s, D_MODEL)


def kernel(x_prompt, x_sample, norm0, w_in0, w_out0, a_q_norm, a_k_norm, b_q_norm, b_k_norm, lambda_q1, lambda_k1, lambda_q2, lambda_k2, b_subln, norm1, w_in1, w_out1, c_q_norm, c_k_norm):
    two = lambda g: jnp.concatenate([g, g]).reshape(1, LANES).astype(F32)
    max_seq = max(x_prompt.shape[1], x_sample.shape[1])
    params = {
        "norm0": norm0.reshape(1, D_MODEL), "norm1": norm1.reshape(1, D_MODEL),
        "w_in0": w_in0.astype(BF16), "w_out0": w_out0.astype(BF16),
        "w_in1": w_in1.astype(BF16), "w_out1": w_out1.astype(BF16),
        "gmat": _head_mean_matrix(),
        "gains0": tuple(two(g) for g in (a_q_norm, a_k_norm, b_q_norm, b_k_norm)),
        "gains1": tuple(two(g) for g in (c_q_norm, c_k_norm)),
        "tables": _rope_tables(max_seq),
        "lams": tuple(t.reshape(1, HEAD_DIM) for t in (lambda_q1, lambda_k1, lambda_q2, lambda_k2)),
        "subln": b_subln.reshape(1, 2 * HEAD_DIM),
        "lam_init": 0.8 - 0.6 * math.exp(-0.3 * 0),
        "small0": _scores_are_small(a_q_norm, a_k_norm) * _scores_are_small(b_q_norm, b_k_norm),
        "small1": _scores_are_small(c_q_norm, c_k_norm),
    }
    return (_trunk(x_prompt, params), _trunk(x_sample, params))
```

```python
import functools
import math

import jax
import jax.numpy as jnp
from jax import lax
from jax.experimental import pallas as pl
from jax.experimental.pallas import tpu as pltpu

D_MODEL = 1024
HEAD_DIM = 64
GRID_W = 64
EPS = 1e-6
NEG = -1e30

A_Q_HEADS = 8
A_KV_HEADS = 2
AXIAL_THETA = 10000.0
B_HEADS = 4
C_HEADS = 16
C_PATTERNS = ((128, 1), (512, 4), (2048, 16))
ROPE_THETA = 500000.0
ROPE_DIMS = HEAD_DIM // 4

A_Q_W = A_Q_HEADS * HEAD_DIM
A_KV_W = A_KV_HEADS * HEAD_DIM
B_QK_W = B_HEADS * 2 * HEAD_DIM
B_V_W = B_HEADS * 2 * HEAD_DIM
EVEN_MIX_W = A_Q_W + B_V_W
C_W = C_HEADS * HEAD_DIM

LANES = 128
NORM_W = 256
VMEM_LIMIT = 56 * 1024 * 1024
PROJ_ROWS = 1024
ATTN_SCORE_ELEMS = 512 * 2048
ATTN_ROWS_ROWMAX = 256
B_HEAD_GROUP = 2
BF16_SUBLANES = 16
A_VT_ROWS = HEAD_DIM + BF16_SUBLANES
BAND_ROWS = 128
BAND_RADIUS = 64
BLOCK_UNROLL = 4
BLOCK_UNROLL_SMALL = 32
SAFE_SCORE = 30.0
NORM_SLACK = 1.02
QK_SCALE = HEAD_DIM ** -0.5

BF16 = jnp.bfloat16
F32 = jnp.float32


def _rope_tables(n_pos):
    pos = jnp.arange(n_pos, dtype=F32)
    lane = jnp.arange(HEAD_DIM)

    def angles(p, n_dims, theta):
        freqs = theta ** (-jnp.arange(0, n_dims, 2, dtype=F32) / n_dims)
        ang = p[:, None] * freqs[None, :]
        return jnp.concatenate([ang, ang], axis=-1)

    half = HEAD_DIM // 2
    row = jnp.floor(pos / GRID_W)
    col = pos - row * GRID_W
    ang = jnp.concatenate([angles(row, half, AXIAL_THETA), angles(col, half, AXIAL_THETA)], axis=-1)
    first = (lane % half) < (half // 2)
    ax_c = jnp.cos(ang)
    ax_sa = jnp.where(first[None, :], -jnp.sin(ang), 0.0)
    ax_sb = jnp.where(first[None, :], 0.0, jnp.sin(ang))

    angp = angles(pos, ROPE_DIMS, ROPE_THETA)
    pad = jnp.zeros((n_pos, HEAD_DIM - ROPE_DIMS), F32)
    cosp = jnp.concatenate([jnp.cos(angp), pad + 1.0], axis=-1)
    sinp = jnp.concatenate([jnp.sin(angp), pad], axis=-1)
    p_first = lane < (ROPE_DIMS // 2)
    p_second = (lane >= ROPE_DIMS // 2) & (lane < ROPE_DIMS)
    pr_c = cosp
    pr_sa = jnp.where(p_first[None, :], -sinp, 0.0)
    pr_sb = jnp.where(p_second[None, :], sinp, 0.0)

    two = lambda t: jnp.concatenate([t, t], axis=-1).astype(F32)
    return tuple(two(t) for t in (ax_c, ax_sa, ax_sb, pr_c, pr_sa, pr_sb))


def _scores_are_small(q_gain, k_gain):
    bound = (HEAD_DIM ** 0.5) * NORM_SLACK * jnp.max(jnp.abs(q_gain)) * jnp.max(jnp.abs(k_gain))
    return (bound <= SAFE_SCORE).astype(jnp.int32).reshape(1)


def _head_mean_matrix():
    i = jnp.arange(NORM_W)
    same = (i[:, None] // HEAD_DIM) == (i[None, :] // HEAD_DIM)
    return jnp.where(same, 1.0 / HEAD_DIM, 0.0).astype(BF16)


def _rms_rows(x, g):
    return x * lax.rsqrt(jnp.mean(x * x, axis=-1, keepdims=True) + EPS) * g


def _scaled_rows(x_ref, g_ref):
    x = x_ref[...]
    return (x * g_ref[...]).astype(BF16), lax.rsqrt(jnp.mean(x * x, axis=-1, keepdims=True) + EPS)


def _wide(t):
    return jnp.concatenate([t, t], axis=1)


def _head_norm_rope(x, gmat, gain, c, sa, sb, half, scale):
    ss = jnp.dot((x * x).astype(BF16), gmat, preferred_element_type=F32)
    y = x * lax.rsqrt(ss + EPS) * gain
    out = y * c + pltpu.roll(y, NORM_W - half, 1) * sa + pltpu.roll(y, half, 1) * sb
    if scale != 1.0:
        out = out * scale
    return out.astype(BF16)


def _proj0_kernel(x_ref, g_ref, w_ref, gmat_ref, aq_ref, ak_ref, bq_ref, bk_ref,
                  axc_ref, axsa_ref, axsb_ref, prc_ref, prsa_ref, prsb_ref,
                  qa_ref, ka_ref, va_ref, qb_ref, kb_ref, vb_ref, sg_ref, vat_ref, vbt_ref):
    h, row_scale = _scaled_rows(x_ref, g_ref)
    gmat = gmat_ref[...]
    ax = tuple(_wide(t[...]) for t in (axc_ref, axsa_ref, axsb_ref))
    pr = tuple(_wide(t[...]) for t in (prc_ref, prsa_ref, prsb_ref))

    def proj(c0, width):
        return jnp.dot(h, w_ref[:, c0:c0 + width], preferred_element_type=F32) * row_scale

    def normed(z, out_ref, gain, tabs, half, scale):
        for c in range(z.shape[1] // NORM_W):
            sl = slice(c * NORM_W, (c + 1) * NORM_W)
            out_ref[:, sl] = _head_norm_rope(z[:, sl], gmat, _wide(gain), *tabs, half, scale)

    c0 = 0
    normed(proj(c0, A_Q_W), qa_ref, aq_ref[...], ax, HEAD_DIM // 4, QK_SCALE)
    c0 += A_Q_W
    zkv = proj(c0, 2 * A_KV_W)
    kv = _head_norm_rope(zkv, gmat, _wide(ak_ref[...]), *ax, HEAD_DIM // 4, 1.0)
    ka_ref[...] = kv[:, :A_KV_W]
    zv = zkv[:, A_KV_W:]
    va_ref[...] = zv.astype(BF16)
    zv_t = zv.T
    ones_rows = jnp.ones((BF16_SUBLANES, zv_t.shape[1]), F32)
    for g in range(A_KV_HEADS):
        vat_ref[g] = jnp.concatenate([zv_t[g * HEAD_DIM:(g + 1) * HEAD_DIM], ones_rows], axis=0).astype(BF16)
    c0 += 2 * A_KV_W
    normed(proj(c0, B_QK_W), qb_ref, bq_ref[...], pr, ROPE_DIMS // 2, QK_SCALE)
    c0 += B_QK_W
    normed(proj(c0, B_QK_W), kb_ref, bk_ref[...], pr, ROPE_DIMS // 2, 1.0)
    c0 += B_QK_W
    zb = proj(c0, B_V_W)
    vb_ref[...] = zb.astype(BF16)
    zb_t = zb.T
    for hb in range(B_HEADS):
        vbt_ref[hb] = zb_t[hb * 2 * HEAD_DIM:(hb + 1) * 2 * HEAD_DIM].astype(BF16)
    c0 += B_V_W
    gate = proj(c0, EVEN_MIX_W)
    sg_ref[...] = (gate * jax.nn.sigmoid(gate)).astype(BF16)


def _row_spec(width, rows=PROJ_ROWS):
    return pl.BlockSpec((rows, width), lambda i: (i, 0))


def _const_spec(shape):
    return pl.BlockSpec(shape, lambda i: (0,) * len(shape))


def _proj0(x2d, seq, norm_g, w_bf16, gmat, gains, tables):
    n = x2d.shape[0]
    pos_blocks = seq // PROJ_ROWS
    tab_spec = pl.BlockSpec((PROJ_ROWS, LANES), lambda i: (i % pos_blocks, 0))
    widths = (A_Q_W, A_KV_W, A_KV_W, B_QK_W, B_QK_W, B_V_W, EVEN_MIX_W)
    t_spec = lambda heads, rows: pl.BlockSpec((None, heads, rows, PROJ_ROWS),
                                              lambda i: (i // pos_blocks, 0, 0, i % pos_blocks))
    t_shape = lambda heads, rows: jax.ShapeDtypeStruct((n // seq, heads, rows, seq), BF16)
    t_dims = ((A_KV_HEADS, A_VT_ROWS), (B_HEADS, 2 * HEAD_DIM))
    return pl.pallas_call(
        _proj0_kernel,
        grid=(n // PROJ_ROWS,),
        in_specs=[_row_spec(D_MODEL), _const_spec((1, D_MODEL)), _const_spec(w_bf16.shape),
                  _const_spec((NORM_W, NORM_W))] + [_const_spec((1, LANES))] * 4 + [tab_spec] * 6,
        out_specs=[_row_spec(w) for w in widths] + [t_spec(*d) for d in t_dims],
        out_shape=[jax.ShapeDtypeStruct((n, w), BF16) for w in widths] + [t_shape(*d) for d in t_dims],
        compiler_params=pltpu.CompilerParams(dimension_semantics=("parallel",),
                                             vmem_limit_bytes=VMEM_LIMIT),
        name="proj0",
    )(x2d, norm_g, w_bf16, gmat, *gains, *tables)


def _scores(q, k):
    return lax.dot_general(q, k, (((1,), (1,)), ((), ())), preferred_element_type=F32)


def _lambda(lam_init, lq1_ref, lk1_ref, lq2_ref, lk2_ref):
    return (jnp.exp(jnp.sum(lq1_ref[...] * lk1_ref[...], axis=-1, keepdims=True))
            - jnp.exp(jnp.sum(lq2_ref[...] * lk2_ref[...], axis=-1, keepdims=True)) + lam_init)


def _attn0_small_body(lam_init, lq1_ref, lk1_ref, lq2_ref, lk2_ref, subln_ref,
                      qa_ref, qb_ref, sg_ref, x_ref, ka_ref, vat_ref, kb_ref, vbt_ref, w_ref, y_ref):
    lam = _lambda(lam_init, lq1_ref, lk1_ref, lq2_ref, lk2_ref)

    def exp_scores_t(k, q):
        return jnp.exp(_scores(k, q))

    outs_t = []
    group = A_Q_HEADS // A_KV_HEADS
    for g in range(A_KV_HEADS):
        k = ka_ref[:, g * HEAD_DIM:(g + 1) * HEAD_DIM]
        vt = vat_ref[g]
        es = [exp_scores_t(k, qa_ref[:, (g * group + j) * HEAD_DIM:(g * group + j + 1) * HEAD_DIM]).astype(BF16)
              for j in range(group)]
        for e in es:
            o2 = jnp.dot(vt, e, preferred_element_type=F32)
            outs_t.append(o2[:HEAD_DIM] * (1.0 / o2[HEAD_DIM:HEAD_DIM + 1]))

    for hb0 in range(0, B_HEADS, B_HEAD_GROUP):
        combined = []
        for hb in range(hb0, hb0 + B_HEAD_GROUP):
            c0 = hb * 2 * HEAD_DIM
            ef = [exp_scores_t(kb_ref[:, c0 + c * HEAD_DIM:c0 + (c + 1) * HEAD_DIM],
                               qb_ref[:, c0 + c * HEAD_DIM:c0 + (c + 1) * HEAD_DIM]) for c in range(2)]
            l0, l1 = [jnp.sum(e, axis=0, keepdims=True) for e in ef]
            a = ef[0].astype(BF16) - ef[1].astype(BF16) * (lam * l0 / l1).astype(BF16)
            combined.append((a, l0))
        for hb, (a, l0) in zip(range(hb0, hb0 + B_HEAD_GROUP), combined):
            outs_t.append(jnp.dot(vbt_ref[hb], a, preferred_element_type=F32) * (1.0 / l0))

    o = jnp.concatenate(outs_t, axis=0).T
    subln = subln_ref[...]
    parts = [o[:, :A_Q_W]]
    for hb in range(B_HEADS):
        c0 = A_Q_W + hb * 2 * HEAD_DIM
        parts.append(_rms_rows(o[:, c0:c0 + 2 * HEAD_DIM], subln) * (1.0 - lam_init))
    mix = (jnp.concatenate(parts, axis=-1) * sg_ref[...].astype(F32)).astype(BF16)
    y_ref[...] = x_ref[...] + jnp.dot(mix, w_ref[...], preferred_element_type=F32)


def _attn0_rowmax_body(lam_init, lq1_ref, lk1_ref, lq2_ref, lk2_ref, subln_ref,
                       qa_ref, qb_ref, sg_ref, x_ref, ka_ref, va_ref, kb_ref, vb_ref, w_ref, y_ref):
    lam = _lambda(lam_init, lq1_ref, lk1_ref, lq2_ref, lk2_ref)
    ones_cols = jnp.ones((ka_ref.shape[0], LANES), BF16)

    def exp_scores(q, k):
        s = _scores(q, k)
        return jnp.exp(s - jnp.max(s, axis=-1, keepdims=True)).astype(BF16)

    outs = []
    group = A_Q_HEADS // A_KV_HEADS
    va_ext = jnp.concatenate([va_ref[...], ones_cols], axis=1)
    for g in range(A_KV_HEADS):
        k = ka_ref[:, g * HEAD_DIM:(g + 1) * HEAD_DIM]
        for j in range(group):
            hq = g * group + j
            o2 = jnp.dot(exp_scores(qa_ref[:, hq * HEAD_DIM:(hq + 1) * HEAD_DIM], k), va_ext,
                         preferred_element_type=F32)
            outs.append(o2[:, g * HEAD_DIM:(g + 1) * HEAD_DIM] * (1.0 / o2[:, LANES:LANES + HEAD_DIM]))

    subln = subln_ref[...]
    for hb in range(B_HEADS):
        c0 = hb * 2 * HEAD_DIM
        vb_ext = jnp.concatenate([vb_ref[:, c0:c0 + 2 * HEAD_DIM], ones_cols], axis=1)
        parts = [jnp.dot(exp_scores(qb_ref[:, c0 + c * HEAD_DIM:c0 + (c + 1) * HEAD_DIM],
                                    kb_ref[:, c0 + c * HEAD_DIM:c0 + (c + 1) * HEAD_DIM]), vb_ext,
                         preferred_element_type=F32) for c in range(2)]
        o = (parts[0][:, :LANES] * (1.0 / parts[0][:, LANES:])
             - parts[1][:, :LANES] * (lam / parts[1][:, LANES:]))
        outs.append(_rms_rows(o, subln) * (1.0 - lam_init))

    mix = (jnp.concatenate(outs, axis=-1) * sg_ref[...].astype(F32)).astype(BF16)
    y_ref[...] = x_ref[...] + jnp.dot(mix, w_ref[...], preferred_element_type=F32)


def _attn0(small, lam_init, x3d, qa, ka, qb, kb, sg, w_out, lams, subln, v_a, v_b):
    b, s, _ = x3d.shape
    rows = ATTN_SCORE_ELEMS // s if small else ATTN_ROWS_ROWMAX
    r3 = lambda t: t.reshape(b, s, t.shape[-1])
    qa, ka, qb, kb, sg = map(r3, (qa, ka, qb, kb, sg))
    row = lambda w: pl.BlockSpec((None, rows, w), lambda bi, i: (bi, i, 0))
    full = lambda w: pl.BlockSpec((None, s, w), lambda bi, i: (bi, 0, 0))
    full_t = lambda heads, t_rows: pl.BlockSpec((None, heads, t_rows, s), lambda bi, i: (bi, 0, 0, 0))
    const = lambda shape: pl.BlockSpec(shape, lambda bi, i: (0,) * len(shape))
    if small:
        body, v_specs = _attn0_small_body, (full_t(A_KV_HEADS, A_VT_ROWS), full_t(B_HEADS, 2 * HEAD_DIM))
    else:
        body, v_specs = _attn0_rowmax_body, (full(A_KV_W), full(B_V_W))
        v_a, v_b = r3(v_a), r3(v_b)
    return pl.pallas_call(
        functools.partial(body, lam_init),
        grid=(b, s // rows),
        in_specs=[const((1, HEAD_DIM))] * 4 + [const((1, 2 * HEAD_DIM)),
                  row(A_Q_W), row(B_QK_W), row(EVEN_MIX_W), row(D_MODEL),
                  full(A_KV_W), v_specs[0], full(B_QK_W), v_specs[1], const(w_out.shape)],
        out_specs=row(D_MODEL),
        out_shape=jax.ShapeDtypeStruct((b, s, D_MODEL), F32),
        compiler_params=pltpu.CompilerParams(dimension_semantics=("parallel", "arbitrary"),
                                             vmem_limit_bytes=VMEM_LIMIT),
        name="attn0" if small else "attn0_rowmax",
    )(*lams, subln, qa, qb, sg, x3d, ka, v_a, kb, v_b, w_out)


def _proj1_kernel(x_ref, g_ref, w_ref, gmat_ref, cq_ref, ck_ref, prc_ref, prsa_ref, prsb_ref,
                  q_ref, k_ref, v_ref, sg_ref):
    h, row_scale = _scaled_rows(x_ref, g_ref)
    gmat = gmat_ref[...]
    pr = tuple(_wide(t[...]) for t in (prc_ref, prsa_ref, prsb_ref))

    def proj(c0, width):
        return jnp.dot(h, w_ref[:, c0:c0 + width], preferred_element_type=F32) * row_scale

    for out_ref, gain_ref, c0, scale in ((q_ref, cq_ref, 0, QK_SCALE), (k_ref, ck_ref, C_W, 1.0)):
        gain = _wide(gain_ref[...])
        for half_w in range(2):
            z = proj(c0 + half_w * (C_W // 2), C_W // 2)
            for c in range(C_W // 2 // NORM_W):
                sl = slice(c * NORM_W, (c + 1) * NORM_W)
                out_ref[:, half_w * (C_W // 2) + c * NORM_W: half_w * (C_W // 2) + (c + 1) * NORM_W] = (
                    _head_norm_rope(z[:, sl], gmat, gain, *pr, ROPE_DIMS // 2, scale))
    v_ref[...] = proj(2 * C_W, C_W).astype(BF16)
    gate = proj(3 * C_W, C_W)
    sg_ref[...] = (gate * jax.nn.sigmoid(gate)).astype(BF16)


def _proj1(x2d, seq, norm_g, w_bf16, gmat, gains, tables):
    n = x2d.shape[0]
    pos_blocks = seq // PROJ_ROWS
    tab_spec = pl.BlockSpec((PROJ_ROWS, LANES), lambda i: (i % pos_blocks, 0))
    return pl.pallas_call(
        _proj1_kernel,
        grid=(n // PROJ_ROWS,),
        in_specs=[_row_spec(D_MODEL), _const_spec((1, D_MODEL)), _const_spec(w_bf16.shape),
                  _const_spec((NORM_W, NORM_W))] + [_const_spec((1, LANES))] * 2 + [tab_spec] * 3,
        out_specs=[_row_spec(C_W)] * 4,
        out_shape=[jax.ShapeDtypeStruct((n, C_W), BF16)] * 4,
        compiler_params=pltpu.CompilerParams(dimension_semantics=("parallel",),
                                             vmem_limit_bytes=VMEM_LIMIT),
        name="proj1",
    )(x2d, norm_g, w_bf16, gmat, *gains, *tables)


def _dilated_kernel(seq, small, q_ref, k_ref, v_ref, sg_ref, o_ref, qf, kf, vf, qd, kd, vd, bias, acc, den, mx=None):
    qf[...] = q_ref[...].astype(F32)
    kf[...] = k_ref[...].astype(F32)
    vf[...] = v_ref[...].astype(F32)

    rr = lax.broadcasted_iota(jnp.int32, (BAND_ROWS, 2 * BAND_ROWS), 0)
    cc = lax.broadcasted_iota(jnp.int32, (BAND_ROWS, 2 * BAND_ROWS), 1)
    band = (cc >= rr) & (cc <= rr + 2 * BAND_RADIUS)
    not_before = cc >= BAND_RADIUS
    not_after = cc < BAND_ROWS + BAND_RADIUS
    for idx, ok in enumerate((band, band & not_before, band & not_after, band & not_before & not_after)):
        bias[idx] = jnp.where(ok, 0.0, NEG).astype(F32)

    lane = lax.broadcasted_iota(jnp.int32, (1, LANES), 1)
    head0 = lane < HEAD_DIM
    zeros_pad = jnp.zeros((BAND_RADIUS, LANES), BF16)
    ones_cols = jnp.ones((2 * BAND_ROWS, LANES), BF16)

    for p, (_, dil) in enumerate(C_PATTERNS):
        length = seq // dil
        padded = length + 2 * BAND_RADIUS
        blocks = length // BAND_ROWS
        q_src = q_ref if dil == 1 else qd
        for r in range(dil):
            if dil == 1:
                k_rows, v_rows = k_ref[...], v_ref[...]
            else:
                rows = pl.ds(r, length, stride=dil)
                qd[r * length:(r + 1) * length, :] = qf[rows, :].astype(BF16)
                k_rows, v_rows = kf[rows, :].astype(BF16), vf[rows, :].astype(BF16)
            kd[r * padded:r * padded + BAND_RADIUS, :] = zeros_pad
            kd[r * padded + BAND_RADIUS:r * padded + BAND_RADIUS + length, :] = k_rows
            kd[r * padded + BAND_RADIUS + length:(r + 1) * padded, :] = zeros_pad
            vd[r * padded:r * padded + BAND_RADIUS, :] = zeros_pad
            vd[r * padded + BAND_RADIUS:r * padded + BAND_RADIUS + length, :] = v_rows
            vd[r * padded + BAND_RADIUS + length:(r + 1) * padded, :] = zeros_pad

        def block(t, carry, dil=dil, length=length, padded=padded, blocks=blocks, p=p, q_src=q_src):
            r = t // blocks
            i = t - r * blocks
            q0 = pl.multiple_of(r * length + i * BAND_ROWS, BAND_ROWS)
            k0 = pl.multiple_of(r * padded + i * BAND_ROWS, BAND_RADIUS)
            qb = q_src[pl.ds(q0, BAND_ROWS), :]
            kb = kd[pl.ds(k0, 2 * BAND_ROWS), :]
            vb = vd[pl.ds(k0, 2 * BAND_ROWS), :]
            which = jnp.where(i == 0, 1, 0) + jnp.where(i == blocks - 1, 2, 0)
            bias_blk = bias[which]
            zq = jnp.zeros_like(qb)
            qs = jnp.concatenate([jnp.where(head0, qb, zq), jnp.where(head0, zq, qb)], axis=0)
            s = lax.dot_general(qs, kb, (((1,), (1,)), ((), ())), preferred_element_type=F32)
            s = s + jnp.concatenate([bias_blk, bias_blk], axis=0)
            if small:
                e = jnp.exp(s).astype(BF16)
            else:
                m = jnp.max(s, axis=-1, keepdims=True)
                e = jnp.exp(s - m).astype(BF16)
            o2 = jnp.dot(e, jnp.concatenate([vb, ones_cols], axis=1), preferred_element_type=F32)
            o = jnp.where(head0, o2[:BAND_ROWS, :LANES], o2[BAND_ROWS:, :LANES])
            l2 = jnp.where(head0, o2[:BAND_ROWS, LANES:], o2[BAND_ROWS:, LANES:])
            if dil > 1:
                dst = pl.ds(i * (BAND_ROWS * dil) + r, BAND_ROWS, stride=dil)
            else:
                dst = pl.ds(q0, BAND_ROWS)
            acc[p, dst, :] = o
            den[p, dst, :] = l2
            if not small:
                mb = jnp.broadcast_to(m, (2 * BAND_ROWS, LANES))
                mx[p, dst, :] = jnp.where(head0, mb[:BAND_ROWS], mb[BAND_ROWS:])
            return carry

        lax.fori_loop(0, dil * blocks, block, 0, unroll=BLOCK_UNROLL_SMALL if small else BLOCK_UNROLL)

    num = jnp.zeros((seq, LANES), F32)
    dsum = jnp.zeros((seq, LANES), F32)
    if small:
        for p in range(len(C_PATTERNS)):
            num = num + acc[p]
            dsum = dsum + den[p]
    else:
        m_all = jnp.maximum(jnp.maximum(mx[0], mx[1]), mx[2])
        for p in range(len(C_PATTERNS)):
            w = jnp.exp(mx[p] - m_all)
            num = num + w * acc[p]
            dsum = dsum + w * den[p]
    o_ref[...] = (num / dsum * sg_ref[...].astype(F32)).astype(BF16)


def _dilated(small, q, k, v, sg, b, s):
    r3 = lambda t: t.reshape(b, s, C_W)
    spec = pl.BlockSpec((None, s, LANES), lambda bi, hp: (bi, 0, hp))
    n_pat = len(C_PATTERNS)
    max_dil = max(d for _, d in C_PATTERNS)
    padded_rows = s + 2 * BAND_RADIUS * max_dil
    return pl.pallas_call(
        functools.partial(_dilated_kernel, s, small),
        grid=(b, C_W // LANES),
        in_specs=[spec] * 4,
        out_specs=spec,
        out_shape=jax.ShapeDtypeStruct((b, s, C_W), BF16),
        scratch_shapes=[pltpu.VMEM((s, LANES), F32)] * 3
        + [pltpu.VMEM((s, LANES), BF16)]
        + [pltpu.VMEM((padded_rows, LANES), BF16)] * 2
        + [pltpu.VMEM((4, BAND_ROWS, 2 * BAND_ROWS), F32)]
        + [pltpu.VMEM((n_pat, s, LANES), F32)] * (2 if small else 3),
        compiler_params=pltpu.CompilerParams(dimension_semantics=("parallel", "arbitrary"),
                                             vmem_limit_bytes=VMEM_LIMIT),
        name="dilated" if small else "dilated_rowmax",
    )(r3(q), r3(k), r3(v), r3(sg))


def _out1_kernel(mix_ref, x_ref, w_ref, y_ref):
    y_ref[...] = x_ref[...] + jnp.dot(mix_ref[...], w_ref[...], preferred_element_type=F32)


def _out1(mix2d, x2d, w_bf16):
    n = x2d.shape[0]
    return pl.pallas_call(
        _out1_kernel,
        grid=(n // PROJ_ROWS,),
        in_specs=[_row_spec(C_W), _row_spec(D_MODEL), _const_spec(w_bf16.shape)],
        out_specs=_row_spec(D_MODEL),
        out_shape=jax.ShapeDtypeStruct((n, D_MODEL), F32),
        compiler_params=pltpu.CompilerParams(dimension_semantics=("parallel",),
                                             vmem_limit_bytes=VMEM_LIMIT),
        name="out1",
    )(mix2d, x2d, w_bf16)


def _trunk(x, p):
    b, s, _ = x.shape
    x2d = x.reshape(b * s, D_MODEL)
    qa, ka, va, qb, kb, vb, sg, vat, vbt = _proj0(x2d, s, p["norm0"], p["w_in0"], p["gmat"], p["gains0"],
                                                  p["tables"])
    attn0 = lambda small: (lambda *ops: _attn0(small, p["lam_init"], *ops[:-4],
                                               *(ops[-2:] if small else ops[-4:-2])))
    y0 = lax.cond(p["small0"][0] > 0, attn0(True), attn0(False),
                  x, qa, ka, qb, kb, sg, p["w_out0"], p["lams"], p["subln"], va, vb, vat, vbt)
    y0_2d = y0.reshape(b * s, D_MODEL)
    q, k, v, sg1 = _proj1(y0_2d, s, p["norm1"], p["w_in1"], p["gmat"], p["gains1"], p["tables"][3:])
    mix = lax.cond(p["small1"][0] > 0, functools.partial(_dilated, True, b=b, s=s),
                   functools.partial(_dilated, False, b=b, s=s), q, k, v, sg1)
    y1 = _out1(mix.reshape(b * s, C_W), y0_2d, p["w_out1"])
    return y1.reshape(b, s, D_MODEL)


def kernel(x_prompt, x_sample, norm0, w_in0, w_out0, a_q_norm, a_k_norm, b_q_norm, b_k_norm, lambda_q1, lambda_k1, lambda_q2, lambda_k2, b_subln, norm1, w_in1, w_out1, c_q_norm, c_k_norm):
    two = lambda g: jnp.concatenate([g, g]).reshape(1, LANES).astype(F32)
    max_seq = max(x_prompt.shape[1], x_sample.shape[1])
    params = {
        "norm0": norm0.reshape(1, D_MODEL), "norm1": norm1.reshape(1, D_MODEL),
        "w_in0": w_in0.astype(BF16), "w_out0": w_out0.astype(BF16),
        "w_in1": w_in1.astype(BF16), "w_out1": w_out1.astype(BF16),
        "gmat": _head_mean_matrix(),
        "gains0": tuple(two(g) for g in (a_q_norm, a_k_norm, b_q_norm, b_k_norm)),
        "gains1": tuple(two(g) for g in (c_q_norm, c_k_norm)),
        "tables": _rope_tables(max_seq),
        "lams": tuple(t.reshape(1, HEAD_DIM) for t in (lambda_q1, lambda_k1, lambda_q2, lambda_k2)),
        "subln": b_subln.reshape(1, 2 * HEAD_DIM),
        "lam_init": 0.8 - 0.6 * math.exp(-0.3 * 0),
        "small0": _scores_are_small(a_q_norm, a_k_norm) * _scores_are_small(b_q_norm, b_k_norm),
        "small1": _scores_are_small(c_q_norm, c_k_norm),
    }
    return (_trunk(x_prompt, params), _trunk(x_sample, params))
```

```python
import functools
import math

import jax
import jax.numpy as jnp
from jax import lax
from jax.experimental import pallas as pl
from jax.experimental.pallas import tpu as pltpu

D_MODEL = 1024
HEAD_DIM = 64
GRID_W = 64
EPS = 1e-6
NEG = -1e30

A_Q_HEADS = 8
A_KV_HEADS = 2
AXIAL_THETA = 10000.0
B_HEADS = 4
C_HEADS = 16
C_PATTERNS = ((128, 1), (512, 4), (2048, 16))
ROPE_THETA = 500000.0
ROPE_DIMS = HEAD_DIM // 4

A_Q_W = A_Q_HEADS * HEAD_DIM
A_KV_W = A_KV_HEADS * HEAD_DIM
B_QK_W = B_HEADS * 2 * HEAD_DIM
B_V_W = B_HEADS * 2 * HEAD_DIM
EVEN_MIX_W = A_Q_W + B_V_W
C_W = C_HEADS * HEAD_DIM

LANES = 128
NORM_W = 256
VMEM_LIMIT = 56 * 1024 * 1024
PROJ_ROWS = 1024
ATTN_SCORE_ELEMS = 512 * 2048
ATTN_ROWS_ROWMAX = 256
B_HEAD_GROUP = 2
BAND_ROWS = 128
BAND_RADIUS = 64
DILATED_PAIR_TOKENS = 2 * 2048
BLOCK_UNROLL = 4
BLOCK_UNROLL_SMALL = 32
SAFE_SCORE = 30.0
NORM_SLACK = 1.02
QK_SCALE = HEAD_DIM ** -0.5

BF16 = jnp.bfloat16
F32 = jnp.float32


def _rope_tables(n_pos):
    pos = jnp.arange(n_pos, dtype=F32)
    lane = jnp.arange(HEAD_DIM)

    def angles(p, n_dims, theta):
        freqs = theta ** (-jnp.arange(0, n_dims, 2, dtype=F32) / n_dims)
        ang = p[:, None] * freqs[None, :]
        return jnp.concatenate([ang, ang], axis=-1)

    half = HEAD_DIM // 2
    row = jnp.floor(pos / GRID_W)
    col = pos - row * GRID_W
    ang = jnp.concatenate([angles(row, half, AXIAL_THETA), angles(col, half, AXIAL_THETA)], axis=-1)
    first = (lane % half) < (half // 2)
    ax_c = jnp.cos(ang)
    ax_sa = jnp.where(first[None, :], -jnp.sin(ang), 0.0)
    ax_sb = jnp.where(first[None, :], 0.0, jnp.sin(ang))

    angp = angles(pos, ROPE_DIMS, ROPE_THETA)
    pad = jnp.zeros((n_pos, HEAD_DIM - ROPE_DIMS), F32)
    cosp = jnp.concatenate([jnp.cos(angp), pad + 1.0], axis=-1)
    sinp = jnp.concatenate([jnp.sin(angp), pad], axis=-1)
    p_first = lane < (ROPE_DIMS // 2)
    p_second = (lane >= ROPE_DIMS // 2) & (lane < ROPE_DIMS)
    pr_c = cosp
    pr_sa = jnp.where(p_first[None, :], -sinp, 0.0)
    pr_sb = jnp.where(p_second[None, :], sinp, 0.0)

    two = lambda t: jnp.concatenate([t, t], axis=-1).astype(F32)
    return tuple(two(t) for t in (ax_c, ax_sa, ax_sb, pr_c, pr_sa, pr_sb))


def _scores_are_small(q_gain, k_gain):
    bound = (HEAD_DIM ** 0.5) * NORM_SLACK * jnp.max(jnp.abs(q_gain)) * jnp.max(jnp.abs(k_gain))
    return (bound <= SAFE_SCORE).astype(jnp.int32).reshape(1)


def _head_mean_matrix():
    i = jnp.arange(NORM_W)
    same = (i[:, None] // HEAD_DIM) == (i[None, :] // HEAD_DIM)
    return jnp.where(same, 1.0 / HEAD_DIM, 0.0).astype(BF16)


def _rms_rows(x, g):
    return x * lax.rsqrt(jnp.mean(x * x, axis=-1, keepdims=True) + EPS) * g


def _scaled_rows(x_ref, g_ref):
    x = x_ref[...]
    return (x * g_ref[...]).astype(BF16), lax.rsqrt(jnp.mean(x * x, axis=-1, keepdims=True) + EPS)


def _wide(t):
    return jnp.concatenate([t, t], axis=1)


def _head_norm_rope(x, gmat, gain, c, sa, sb, half, scale):
    ss = jnp.dot((x * x).astype(BF16), gmat, preferred_element_type=F32)
    y = x * lax.rsqrt(ss + EPS) * gain
    out = y * c + pltpu.roll(y, NORM_W - half, 1) * sa + pltpu.roll(y, half, 1) * sb
    if scale != 1.0:
        out = out * scale
    return out.astype(BF16)


def _proj0_kernel(x_ref, g_ref, w_ref, gmat_ref, aq_ref, ak_ref, bq_ref, bk_ref,
                  axc_ref, axsa_ref, axsb_ref, prc_ref, prsa_ref, prsb_ref,
                  qa_ref, ka_ref, va_ref, qb_ref, kb_ref, vb_ref, sg_ref, vat_ref, vbt_ref):
    h, row_scale = _scaled_rows(x_ref, g_ref)
    gmat = gmat_ref[...]
    ax = tuple(_wide(t[...]) for t in (axc_ref, axsa_ref, axsb_ref))
    pr = tuple(_wide(t[...]) for t in (prc_ref, prsa_ref, prsb_ref))

    def proj(c0, width):
        return jnp.dot(h, w_ref[:, c0:c0 + width], preferred_element_type=F32) * row_scale

    def normed(z, out_ref, gain, tabs, half, scale):
        for c in range(z.shape[1] // NORM_W):
            sl = slice(c * NORM_W, (c + 1) * NORM_W)
            out_ref[:, sl] = _head_norm_rope(z[:, sl], gmat, _wide(gain), *tabs, half, scale)

    c0 = 0
    normed(proj(c0, A_Q_W), qa_ref, aq_ref[...], ax, HEAD_DIM // 4, QK_SCALE)
    c0 += A_Q_W
    zkv = proj(c0, 2 * A_KV_W)
    kv = _head_norm_rope(zkv, gmat, _wide(ak_ref[...]), *ax, HEAD_DIM // 4, 1.0)
    ka_ref[...] = kv[:, :A_KV_W]
    zv = zkv[:, A_KV_W:]
    va_ref[...] = zv.astype(BF16)
    zv_t = zv.T
    ones_rows = jnp.ones((HEAD_DIM, zv_t.shape[1]), F32)
    for g in range(A_KV_HEADS):
        vat_ref[g] = jnp.concatenate([zv_t[g * HEAD_DIM:(g + 1) * HEAD_DIM], ones_rows], axis=0).astype(BF16)
    c0 += 2 * A_KV_W
    normed(proj(c0, B_QK_W), qb_ref, bq_ref[...], pr, ROPE_DIMS // 2, QK_SCALE)
    c0 += B_QK_W
    normed(proj(c0, B_QK_W), kb_ref, bk_ref[...], pr, ROPE_DIMS // 2, 1.0)
    c0 += B_QK_W
    zb = proj(c0, B_V_W)
    vb_ref[...] = zb.astype(BF16)
    zb_t = zb.T
    for hb in range(B_HEADS):
        vbt_ref[hb] = zb_t[hb * 2 * HEAD_DIM:(hb + 1) * 2 * HEAD_DIM].astype(BF16)
    c0 += B_V_W
    gate = proj(c0, EVEN_MIX_W)
    sg_ref[...] = (gate * jax.nn.sigmoid(gate)).astype(BF16)


def _row_spec(width, rows=PROJ_ROWS):
    return pl.BlockSpec((rows, width), lambda i: (i, 0))


def _const_spec(shape):
    return pl.BlockSpec(shape, lambda i: (0,) * len(shape))


def _proj0(x2d, seq, norm_g, w_bf16, gmat, gains, tables):
    n = x2d.shape[0]
    pos_blocks = seq // PROJ_ROWS
    tab_spec = pl.BlockSpec((PROJ_ROWS, LANES), lambda i: (i % pos_blocks, 0))
    widths = (A_Q_W, A_KV_W, A_KV_W, B_QK_W, B_QK_W, B_V_W, EVEN_MIX_W)
    t_spec = lambda heads: pl.BlockSpec((None, heads, LANES, PROJ_ROWS),
                                        lambda i: (i // pos_blocks, 0, 0, i % pos_blocks))
    t_shape = lambda heads: jax.ShapeDtypeStruct((n // seq, heads, LANES, seq), BF16)
    return pl.pallas_call(
        _proj0_kernel,
        grid=(n // PROJ_ROWS,),
        in_specs=[_row_spec(D_MODEL), _const_spec((1, D_MODEL)), _const_spec(w_bf16.shape),
                  _const_spec((NORM_W, NORM_W))] + [_const_spec((1, LANES))] * 4 + [tab_spec] * 6,
        out_specs=[_row_spec(w) for w in widths] + [t_spec(A_KV_HEADS), t_spec(B_HEADS)],
        out_shape=[jax.ShapeDtypeStruct((n, w), BF16) for w in widths] + [t_shape(A_KV_HEADS), t_shape(B_HEADS)],
        compiler_params=pltpu.CompilerParams(dimension_semantics=("parallel",),
                                             vmem_limit_bytes=VMEM_LIMIT),
        name="proj0",
    )(x2d, norm_g, w_bf16, gmat, *gains, *tables)


def _scores(q, k):
    return lax.dot_general(q, k, (((1,), (1,)), ((), ())), preferred_element_type=F32)


def _lambda(lam_init, lq1_ref, lk1_ref, lq2_ref, lk2_ref):
    return (jnp.exp(jnp.sum(lq1_ref[...] * lk1_ref[...], axis=-1, keepdims=True))
            - jnp.exp(jnp.sum(lq2_ref[...] * lk2_ref[...], axis=-1, keepdims=True)) + lam_init)


def _attn0_small_body(lam_init, lq1_ref, lk1_ref, lq2_ref, lk2_ref, subln_ref,
                      qa_ref, qb_ref, sg_ref, x_ref, ka_ref, vat_ref, kb_ref, vbt_ref, w_ref, y_ref):
    lam = _lambda(lam_init, lq1_ref, lk1_ref, lq2_ref, lk2_ref)

    def exp_scores_t(k, q):
        return jnp.exp(_scores(k, q))

    outs_t = []
    group = A_Q_HEADS // A_KV_HEADS
    for g in range(A_KV_HEADS):
        k = ka_ref[:, g * HEAD_DIM:(g + 1) * HEAD_DIM]
        vt = vat_ref[g]
        es = [exp_scores_t(k, qa_ref[:, (g * group + j) * HEAD_DIM:(g * group + j + 1) * HEAD_DIM]).astype(BF16)
              for j in range(group)]
        for e in es:
            o2 = jnp.dot(vt, e, preferred_element_type=F32)
            outs_t.append(o2[:HEAD_DIM] * (1.0 / o2[HEAD_DIM:]))

    for hb0 in range(0, B_HEADS, B_HEAD_GROUP):
        combined = []
        for hb in range(hb0, hb0 + B_HEAD_GROUP):
            c0 = hb * 2 * HEAD_DIM
            ef = [exp_scores_t(kb_ref[:, c0 + c * HEAD_DIM:c0 + (c + 1) * HEAD_DIM],
                               qb_ref[:, c0 + c * HEAD_DIM:c0 + (c + 1) * HEAD_DIM]) for c in range(2)]
            l0, l1 = [jnp.sum(e, axis=0, keepdims=True) for e in ef]
            a = ef[0].astype(BF16) - ef[1].astype(BF16) * (lam * l0 / l1).astype(BF16)
            combined.append((a, l0))
        for hb, (a, l0) in zip(range(hb0, hb0 + B_HEAD_GROUP), combined):
            outs_t.append(jnp.dot(vbt_ref[hb], a, preferred_element_type=F32) * (1.0 / l0))

    o = jnp.concatenate(outs_t, axis=0).T
    subln = subln_ref[...]
    parts = [o[:, :A_Q_W]]
    for hb in range(B_HEADS):
        c0 = A_Q_W + hb * 2 * HEAD_DIM
        parts.append(_rms_rows(o[:, c0:c0 + 2 * HEAD_DIM], subln) * (1.0 - lam_init))
    mix = (jnp.concatenate(parts, axis=-1) * sg_ref[...].astype(F32)).astype(BF16)
    y_ref[...] = x_ref[...] + jnp.dot(mix, w_ref[...], preferred_element_type=F32)


def _attn0_rowmax_body(lam_init, lq1_ref, lk1_ref, lq2_ref, lk2_ref, subln_ref,
                       qa_ref, qb_ref, sg_ref, x_ref, ka_ref, va_ref, kb_ref, vb_ref, w_ref, y_ref):
    lam = _lambda(lam_init, lq1_ref, lk1_ref, lq2_ref, lk2_ref)
    ones_cols = jnp.ones((ka_ref.shape[0], LANES), BF16)

    def exp_scores(q, k):
        s = _scores(q, k)
        return jnp.exp(s - jnp.max(s, axis=-1, keepdims=True)).astype(BF16)

    outs = []
    group = A_Q_HEADS // A_KV_HEADS
    va_ext = jnp.concatenate([va_ref[...], ones_cols], axis=1)
    for g in range(A_KV_HEADS):
        k = ka_ref[:, g * HEAD_DIM:(g + 1) * HEAD_DIM]
        for j in range(group):
            hq = g * group + j
            o2 = jnp.dot(exp_scores(qa_ref[:, hq * HEAD_DIM:(hq + 1) * HEAD_DIM], k), va_ext,
                         preferred_element_type=F32)
            outs.append(o2[:, g * HEAD_DIM:(g + 1) * HEAD_DIM] * (1.0 / o2[:, LANES:LANES + HEAD_DIM]))

    subln = subln_ref[...]
    for hb in range(B_HEADS):
        c0 = hb * 2 * HEAD_DIM
        vb_ext = jnp.concatenate([vb_ref[:, c0:c0 + 2 * HEAD_DIM], ones_cols], axis=1)
        parts = [jnp.dot(exp_scores(qb_ref[:, c0 + c * HEAD_DIM:c0 + (c + 1) * HEAD_DIM],
                                    kb_ref[:, c0 + c * HEAD_DIM:c0 + (c + 1) * HEAD_DIM]), vb_ext,
                         preferred_element_type=F32) for c in range(2)]
        o = (parts[0][:, :LANES] * (1.0 / parts[0][:, LANES:])
             - parts[1][:, :LANES] * (lam / parts[1][:, LANES:]))
        outs.append(_rms_rows(o, subln) * (1.0 - lam_init))

    mix = (jnp.concatenate(outs, axis=-1) * sg_ref[...].astype(F32)).astype(BF16)
    y_ref[...] = x_ref[...] + jnp.dot(mix, w_ref[...], preferred_element_type=F32)


def _attn0(small, lam_init, x3d, qa, ka, qb, kb, sg, w_out, lams, subln, v_a, v_b):
    b, s, _ = x3d.shape
    rows = ATTN_SCORE_ELEMS // s if small else ATTN_ROWS_ROWMAX
    r3 = lambda t: t.reshape(b, s, t.shape[-1])
    qa, ka, qb, kb, sg = map(r3, (qa, ka, qb, kb, sg))
    row = lambda w: pl.BlockSpec((None, rows, w), lambda bi, i: (bi, i, 0))
    full = lambda w: pl.BlockSpec((None, s, w), lambda bi, i: (bi, 0, 0))
    full_t = lambda heads: pl.BlockSpec((None, heads, LANES, s), lambda bi, i: (bi, 0, 0, 0))
    const = lambda shape: pl.BlockSpec(shape, lambda bi, i: (0,) * len(shape))
    if small:
        body, v_specs = _attn0_small_body, (full_t(A_KV_HEADS), full_t(B_HEADS))
    else:
        body, v_specs = _attn0_rowmax_body, (full(A_KV_W), full(B_V_W))
        v_a, v_b = r3(v_a), r3(v_b)
    return pl.pallas_call(
        functools.partial(body, lam_init),
        grid=(b, s // rows),
        in_specs=[const((1, HEAD_DIM))] * 4 + [const((1, 2 * HEAD_DIM)),
                  row(A_Q_W), row(B_QK_W), row(EVEN_MIX_W), row(D_MODEL),
                  full(A_KV_W), v_specs[0], full(B_QK_W), v_specs[1], const(w_out.shape)],
        out_specs=row(D_MODEL),
        out_shape=jax.ShapeDtypeStruct((b, s, D_MODEL), F32),
        compiler_params=pltpu.CompilerParams(dimension_semantics=("parallel", "arbitrary"),
                                             vmem_limit_bytes=VMEM_LIMIT),
        name="attn0" if small else "attn0_rowmax",
    )(*lams, subln, qa, qb, sg, x3d, ka, v_a, kb, v_b, w_out)


def _proj1_kernel(x_ref, g_ref, w_ref, gmat_ref, cq_ref, ck_ref, prc_ref, prsa_ref, prsb_ref,
                  q_ref, k_ref, v_ref, sg_ref):
    h, row_scale = _scaled_rows(x_ref, g_ref)
    gmat = gmat_ref[...]
    pr = tuple(_wide(t[...]) for t in (prc_ref, prsa_ref, prsb_ref))

    def proj(c0, width):
        return jnp.dot(h, w_ref[:, c0:c0 + width], preferred_element_type=F32) * row_scale

    for out_ref, gain_ref, c0, scale in ((q_ref, cq_ref, 0, QK_SCALE), (k_ref, ck_ref, C_W, 1.0)):
        gain = _wide(gain_ref[...])
        for half_w in range(2):
            z = proj(c0 + half_w * (C_W // 2), C_W // 2)
            for c in range(C_W // 2 // NORM_W):
                sl = slice(c * NORM_W, (c + 1) * NORM_W)
                out_ref[:, half_w * (C_W // 2) + c * NORM_W: half_w * (C_W // 2) + (c + 1) * NORM_W] = (
                    _head_norm_rope(z[:, sl], gmat, gain, *pr, ROPE_DIMS // 2, scale))
    v_ref[...] = proj(2 * C_W, C_W).astype(BF16)
    gate = proj(3 * C_W, C_W)
    sg_ref[...] = (gate * jax.nn.sigmoid(gate)).astype(BF16)


def _proj1(x2d, seq, norm_g, w_bf16, gmat, gains, tables):
    n = x2d.shape[0]
    pos_blocks = seq // PROJ_ROWS
    tab_spec = pl.BlockSpec((PROJ_ROWS, LANES), lambda i: (i % pos_blocks, 0))
    return pl.pallas_call(
        _proj1_kernel,
        grid=(n // PROJ_ROWS,),
        in_specs=[_row_spec(D_MODEL), _const_spec((1, D_MODEL)), _const_spec(w_bf16.shape),
                  _const_spec((NORM_W, NORM_W))] + [_const_spec((1, LANES))] * 2 + [tab_spec] * 3,
        out_specs=[_row_spec(C_W)] * 4,
        out_shape=[jax.ShapeDtypeStruct((n, C_W), BF16)] * 4,
        compiler_params=pltpu.CompilerParams(dimension_semantics=("parallel",),
                                             vmem_limit_bytes=VMEM_LIMIT),
        name="proj1",
    )(x2d, norm_g, w_bf16, gmat, *gains, *tables)


def _dilated_kernel(seq, small, pairs, q_ref, k_ref, v_ref, o_ref, bias, *scratch):
    rr = lax.broadcasted_iota(jnp.int32, (BAND_ROWS, 2 * BAND_ROWS), 0)
    cc = lax.broadcasted_iota(jnp.int32, (BAND_ROWS, 2 * BAND_ROWS), 1)
    band = (cc >= rr) & (cc <= rr + 2 * BAND_RADIUS)
    not_before = cc >= BAND_RADIUS
    not_after = cc < BAND_ROWS + BAND_RADIUS
    for idx, ok in enumerate((band, band & not_before, band & not_after, band & not_before & not_after)):
        bias[idx] = jnp.where(ok, 0.0, NEG).astype(F32)

    per_pair = len(scratch) // pairs
    for j in range(pairs):
        lanes = slice(j * LANES, (j + 1) * LANES)
        _dilated_pair(seq, small, q_ref.at[:, lanes], k_ref.at[:, lanes], v_ref.at[:, lanes], o_ref.at[:, lanes],
                      bias, *scratch[j * per_pair:(j + 1) * per_pair])


def _dilated_pair(seq, small, q_ref, k_ref, v_ref, o_ref, bias, qf, kf, vf, qd, kd, vd, acc, den, mx=None):
    qf[...] = q_ref[...].astype(F32)
    kf[...] = k_ref[...].astype(F32)
    vf[...] = v_ref[...].astype(F32)

    lane = lax.broadcasted_iota(jnp.int32, (1, LANES), 1)
    head0 = lane < HEAD_DIM
    zeros_pad = jnp.zeros((BAND_RADIUS, LANES), BF16)
    ones_cols = jnp.ones((2 * BAND_ROWS, LANES), BF16)

    for p, (_, dil) in enumerate(C_PATTERNS):
        length = seq // dil
        padded = length + 2 * BAND_RADIUS
        blocks = length // BAND_ROWS
        q_src = q_ref if dil == 1 else qd
        for r in range(dil):
            if dil == 1:
                k_rows, v_rows = k_ref[...], v_ref[...]
            else:
                rows = pl.ds(r, length, stride=dil)
                qd[r * length:(r + 1) * length, :] = qf[rows, :].astype(BF16)
                k_rows, v_rows = kf[rows, :].astype(BF16), vf[rows, :].astype(BF16)
            kd[r * padded:r * padded + BAND_RADIUS, :] = zeros_pad
            kd[r * padded + BAND_RADIUS:r * padded + BAND_RADIUS + length, :] = k_rows
            kd[r * padded + BAND_RADIUS + length:(r + 1) * padded, :] = zeros_pad
            vd[r * padded:r * padded + BAND_RADIUS, :] = zeros_pad
            vd[r * padded + BAND_RADIUS:r * padded + BAND_RADIUS + length, :] = v_rows
            vd[r * padded + BAND_RADIUS + length:(r + 1) * padded, :] = zeros_pad

        def block(t, carry, dil=dil, length=length, padded=padded, blocks=blocks, p=p, q_src=q_src):
            r = t // blocks
            i = t - r * blocks
            q0 = pl.multiple_of(r * length + i * BAND_ROWS, BAND_ROWS)
            k0 = pl.multiple_of(r * padded + i * BAND_ROWS, BAND_RADIUS)
            qb = q_src[pl.ds(q0, BAND_ROWS), :]
            kb = kd[pl.ds(k0, 2 * BAND_ROWS), :]
            vb = vd[pl.ds(k0, 2 * BAND_ROWS), :]
            which = jnp.where(i == 0, 1, 0) + jnp.where(i == blocks - 1, 2, 0)
            bias_blk = bias[which]
            zq = jnp.zeros_like(qb)
            qs = jnp.concatenate([jnp.where(head0, qb, zq), jnp.where(head0, zq, qb)], axis=0)
            s = lax.dot_general(qs, kb, (((1,), (1,)), ((), ())), preferred_element_type=F32)
            s = s + jnp.concatenate([bias_blk, bias_blk], axis=0)
            if small:
                e = jnp.exp(s).astype(BF16)
            else:
                m = jnp.max(s, axis=-1, keepdims=True)
                e = jnp.exp(s - m).astype(BF16)
            o2 = jnp.dot(e, jnp.concatenate([vb, ones_cols], axis=1), preferred_element_type=F32)
            o = jnp.where(head0, o2[:BAND_ROWS, :LANES], o2[BAND_ROWS:, :LANES])
            l2 = jnp.where(head0, o2[:BAND_ROWS, LANES:], o2[BAND_ROWS:, LANES:])
            if dil > 1:
                dst = pl.ds(i * (BAND_ROWS * dil) + r, BAND_ROWS, stride=dil)
            else:
                dst = pl.ds(q0, BAND_ROWS)
            acc[p, dst, :] = o
            den[p, dst, :] = l2
            if not small:
                mb = jnp.broadcast_to(m, (2 * BAND_ROWS, LANES))
                mx[p, dst, :] = jnp.where(head0, mb[:BAND_ROWS], mb[BAND_ROWS:])
            return carry

        lax.fori_loop(0, dil * blocks, block, 0, unroll=BLOCK_UNROLL_SMALL if small else BLOCK_UNROLL)

    num = jnp.zeros((seq, LANES), F32)
    dsum = jnp.zeros((seq, LANES), F32)
    if small:
        for p in range(len(C_PATTERNS)):
            num = num + acc[p]
            dsum = dsum + den[p]
    else:
        m_all = jnp.maximum(jnp.maximum(mx[0], mx[1]), mx[2])
        for p in range(len(C_PATTERNS)):
            w = jnp.exp(mx[p] - m_all)
            num = num + w * acc[p]
            dsum = dsum + w * den[p]
    o_ref[...] = (num / dsum).astype(BF16)


def _dilated(small, q, k, v, b, s):
    r3 = lambda t: t.reshape(b, s, C_W)
    pairs = max(1, DILATED_PAIR_TOKENS // s) if small else 1
    spec = pl.BlockSpec((None, s, pairs * LANES), lambda bi, hp: (bi, 0, hp))
    n_pat = len(C_PATTERNS)
    max_dil = max(d for _, d in C_PATTERNS)
    padded_rows = s + 2 * BAND_RADIUS * max_dil
    pair_scratch = ([pltpu.VMEM((s, LANES), F32)] * 3
                    + [pltpu.VMEM((s, LANES), BF16)]
                    + [pltpu.VMEM((padded_rows, LANES), BF16)] * 2
                    + [pltpu.VMEM((n_pat, s, LANES), F32)] * (2 if small else 3))
    return pl.pallas_call(
        functools.partial(_dilated_kernel, s, small, pairs),
        grid=(b, C_W // (pairs * LANES)),
        in_specs=[spec] * 3,
        out_specs=spec,
        out_shape=jax.ShapeDtypeStruct((b, s, C_W), BF16),
        scratch_shapes=[pltpu.VMEM((4, BAND_ROWS, 2 * BAND_ROWS), F32)] + pair_scratch * pairs,
        compiler_params=pltpu.CompilerParams(dimension_semantics=("parallel", "arbitrary"),
                                             vmem_limit_bytes=VMEM_LIMIT),
        name="dilated" if small else "dilated_rowmax",
    )(r3(q), r3(k), r3(v))


def _out1_kernel(o_ref, sg_ref, x_ref, w_ref, y_ref):
    mix = (o_ref[...].astype(F32) * sg_ref[...].astype(F32)).astype(BF16)
    y_ref[...] = x_ref[...] + jnp.dot(mix, w_ref[...], preferred_element_type=F32)


def _out1(o2d, sg, x2d, w_bf16):
    n = x2d.shape[0]
    return pl.pallas_call(
        _out1_kernel,
        grid=(n // PROJ_ROWS,),
        in_specs=[_row_spec(C_W), _row_spec(C_W), _row_spec(D_MODEL), _const_spec(w_bf16.shape)],
        out_specs=_row_spec(D_MODEL),
        out_shape=jax.ShapeDtypeStruct((n, D_MODEL), F32),
        compiler_params=pltpu.CompilerParams(dimension_semantics=("parallel",),
                                             vmem_limit_bytes=VMEM_LIMIT),
        name="out1",
    )(o2d, sg, x2d, w_bf16)


def _trunk(x, p):
    b, s, _ = x.shape
    x2d = x.reshape(b * s, D_MODEL)
    qa, ka, va, qb, kb, vb, sg, vat, vbt = _proj0(x2d, s, p["norm0"], p["w_in0"], p["gmat"], p["gains0"],
                                                  p["tables"])
    attn0 = lambda small: (lambda *ops: _attn0(small, p["lam_init"], *ops[:-4],
                                               *(ops[-2:] if small else ops[-4:-2])))
    y0 = lax.cond(p["small0"][0] > 0, attn0(True), attn0(False),
                  x, qa, ka, qb, kb, sg, p["w_out0"], p["lams"], p["subln"], va, vb, vat, vbt)
    y0_2d = y0.reshape(b * s, D_MODEL)
    q, k, v, sg1 = _proj1(y0_2d, s, p["norm1"], p["w_in1"], p["gmat"], p["gains1"], p["tables"][3:])
    o = lax.cond(p["small1"][0] > 0, functools.partial(_dilated, True, b=b, s=s),
                 functools.partial(_dilated, False, b=b, s=s), q, k, v)
    y1 = _out1(o.reshape(b * s, C_W), sg1, y0_2d, p["w_out1"])
    return y1.reshape(b, s, D_MODEL)


def kernel(x_prompt, x_sample, norm0, w_in0, w_out0, a_q_norm, a_k_norm, b_q_norm, b_k_norm, lambda_q1, lambda_k1, lambda_q2, lambda_k2, b_subln, norm1, w_in1, w_out1, c_q_norm, c_k_norm):
    two = lambda g: jnp.concatenate([g, g]).reshape(1, LANES).astype(F32)
    max_seq = max(x_prompt.shape[1], x_sample.shape[1])
    params = {
        "norm0": norm0.reshape(1, D_MODEL), "norm1": norm1.reshape(1, D_MODEL),
        "w_in0": w_in0.astype(BF16), "w_out0": w_out0.astype(BF16),
        "w_in1": w_in1.astype(BF16), "w_out1": w_out1.astype(BF16),
        "gmat": _head_mean_matrix(),
        "gains0": tuple(two(g) for g in (a_q_norm, a_k_norm, b_q_norm, b_k_norm)),
        "gains1": tuple(two(g) for g in (c_q_norm, c_k_norm)),
        "tables": _rope_tables(max_seq),
        "lams": tuple(t.reshape(1, HEAD_DIM) for t in (lambda_q1, lambda_k1, lambda_q2, lambda_k2)),
        "subln": b_subln.reshape(1, 2 * HEAD_DIM),
        "lam_init": 0.8 - 0.6 * math.exp(-0.3 * 0),
        "small0": _scores_are_small(a_q_norm, a_k_norm) * _scores_are_small(b_q_norm, b_k_norm),
        "small1": _scores_are_small(c_q_norm, c_k_norm),
    }
    return (_trunk(x_prompt, params), _trunk(x_sample, params))
```

```python
import functools
import math

import jax
import jax.numpy as jnp
import numpy as np
from jax import lax
from jax.experimental import pallas as pl
from jax.experimental.pallas import tpu as pltpu

D_MODEL = 1024
HEAD_DIM = 64
GRID_W = 64
EPS = 1e-6
NEG = -1e30

A_Q_HEADS = 8
A_KV_HEADS = 2
AXIAL_THETA = 10000.0
B_HEADS = 4
C_HEADS = 16
C_PATTERNS = ((128, 1), (512, 4), (2048, 16))
ROPE_THETA = 500000.0
ROPE_DIMS = HEAD_DIM // 4

A_Q_W = A_Q_HEADS * HEAD_DIM
A_KV_W = A_KV_HEADS * HEAD_DIM
B_QK_W = B_HEADS * 2 * HEAD_DIM
B_V_W = B_HEADS * 2 * HEAD_DIM
EVEN_MIX_W = A_Q_W + B_V_W
C_W = C_HEADS * HEAD_DIM

LANES = 128
NORM_W = 256
VMEM_LIMIT = 56 * 1024 * 1024
PROJ_ROWS = 1024
ATTN_SCORE_ELEMS = 512 * 2048
ATTN_ROWS_ROWMAX = 256
B_HEAD_GROUP = 2
BAND_ROWS = 128
BAND_RADIUS = 64
BLOCK_UNROLL = 4
BLOCK_UNROLL_SMALL = 32
SAFE_SCORE = 30.0
NORM_SLACK = 1.02
QK_SCALE = HEAD_DIM ** -0.5

BF16 = jnp.bfloat16
F32 = jnp.float32


def _rope_tables(n_pos):
    f32 = np.float32
    pos = np.arange(n_pos, dtype=f32)
    lane = np.arange(HEAD_DIM)

    def angles(p, n_dims, theta):
        freqs = np.power(f32(theta), -np.arange(0, n_dims, 2, dtype=f32) / f32(n_dims)).astype(f32)
        ang = (p[:, None] * freqs[None, :]).astype(f32)
        return np.concatenate([ang, ang], axis=-1)

    half = HEAD_DIM // 2
    row = np.floor(pos / f32(GRID_W)).astype(f32)
    col = pos - row * f32(GRID_W)
    ang = np.concatenate([angles(row, half, AXIAL_THETA), angles(col, half, AXIAL_THETA)], axis=-1)
    first = (lane % half) < (half // 2)
    ax_c = np.cos(ang)
    ax_sa = np.where(first[None, :], -np.sin(ang), 0.0)
    ax_sb = np.where(first[None, :], 0.0, np.sin(ang))

    angp = angles(pos, ROPE_DIMS, ROPE_THETA)
    pad = np.zeros((n_pos, HEAD_DIM - ROPE_DIMS), f32)
    cosp = np.concatenate([np.cos(angp), pad + 1.0], axis=-1)
    sinp = np.concatenate([np.sin(angp), pad], axis=-1)
    p_first = lane < (ROPE_DIMS // 2)
    p_second = (lane >= ROPE_DIMS // 2) & (lane < ROPE_DIMS)
    pr_c = cosp
    pr_sa = np.where(p_first[None, :], -sinp, 0.0)
    pr_sb = np.where(p_second[None, :], sinp, 0.0)

    two = lambda t: np.concatenate([t, t], axis=-1).astype(f32)
    return tuple(two(t) for t in (ax_c, ax_sa, ax_sb, pr_c, pr_sa, pr_sb))


def _scores_are_small(q_gain, k_gain):
    bound = (HEAD_DIM ** 0.5) * NORM_SLACK * jnp.max(jnp.abs(q_gain)) * jnp.max(jnp.abs(k_gain))
    return (bound <= SAFE_SCORE).astype(jnp.int32).reshape(1)


def _head_mean_matrix():
    i = np.arange(NORM_W)
    same = (i[:, None] // HEAD_DIM) == (i[None, :] // HEAD_DIM)
    return jnp.asarray(np.where(same, 1.0 / HEAD_DIM, 0.0), dtype=BF16)


def _rms_rows(x, g):
    return x * lax.rsqrt(jnp.mean(x * x, axis=-1, keepdims=True) + EPS) * g


def _scaled_rows(x_ref, g_ref):
    x = x_ref[...]
    return (x * g_ref[...]).astype(BF16), lax.rsqrt(jnp.mean(x * x, axis=-1, keepdims=True) + EPS)


def _wide(t):
    return jnp.concatenate([t, t], axis=1)


def _head_norm_rope(x, gmat, gain, c, sa, sb, half, scale):
    ss = jnp.dot((x * x).astype(BF16), gmat, preferred_element_type=F32)
    y = x * lax.rsqrt(ss + EPS) * gain
    out = y * c + pltpu.roll(y, NORM_W - half, 1) * sa + pltpu.roll(y, half, 1) * sb
    if scale != 1.0:
        out = out * scale
    return out.astype(BF16)


def _proj0_kernel(x_ref, g_ref, w_ref, gmat_ref, aq_ref, ak_ref, bq_ref, bk_ref,
                  axc_ref, axsa_ref, axsb_ref, prc_ref, prsa_ref, prsb_ref,
                  qa_ref, ka_ref, va_ref, qb_ref, kb_ref, vb_ref, sg_ref, vat_ref, vbt_ref):
    h, row_scale = _scaled_rows(x_ref, g_ref)
    gmat = gmat_ref[...]
    ax = tuple(_wide(t[...]) for t in (axc_ref, axsa_ref, axsb_ref))
    pr = tuple(_wide(t[...]) for t in (prc_ref, prsa_ref, prsb_ref))

    def proj(c0, width):
        return jnp.dot(h, w_ref[:, c0:c0 + width], preferred_element_type=F32) * row_scale

    def normed(z, out_ref, gain, tabs, half, scale):
        for c in range(z.shape[1] // NORM_W):
            sl = slice(c * NORM_W, (c + 1) * NORM_W)
            out_ref[:, sl] = _head_norm_rope(z[:, sl], gmat, _wide(gain), *tabs, half, scale)

    c0 = 0
    normed(proj(c0, A_Q_W), qa_ref, aq_ref[...], ax, HEAD_DIM // 4, QK_SCALE)
    c0 += A_Q_W
    zkv = proj(c0, 2 * A_KV_W)
    kv = _head_norm_rope(zkv, gmat, _wide(ak_ref[...]), *ax, HEAD_DIM // 4, 1.0)
    ka_ref[...] = kv[:, :A_KV_W]
    zv = zkv[:, A_KV_W:]
    va_ref[...] = zv.astype(BF16)
    zv_t = zv.T
    ones_rows = jnp.ones((HEAD_DIM, zv_t.shape[1]), F32)
    for g in range(A_KV_HEADS):
        vat_ref[g] = jnp.concatenate([zv_t[g * HEAD_DIM:(g + 1) * HEAD_DIM], ones_rows], axis=0).astype(BF16)
    c0 += 2 * A_KV_W
    normed(proj(c0, B_QK_W), qb_ref, bq_ref[...], pr, ROPE_DIMS // 2, QK_SCALE)
    c0 += B_QK_W
    normed(proj(c0, B_QK_W), kb_ref, bk_ref[...], pr, ROPE_DIMS // 2, 1.0)
    c0 += B_QK_W
    zb = proj(c0, B_V_W)
    vb_ref[...] = zb.astype(BF16)
    zb_t = zb.T
    for hb in range(B_HEADS):
        vbt_ref[hb] = zb_t[hb * 2 * HEAD_DIM:(hb + 1) * 2 * HEAD_DIM].astype(BF16)
    c0 += B_V_W
    gate = proj(c0, EVEN_MIX_W)
    sg_ref[...] = (gate * jax.nn.sigmoid(gate)).astype(BF16)


def _row_spec(width, rows=PROJ_ROWS):
    return pl.BlockSpec((rows, width), lambda i: (i, 0))


def _const_spec(shape):
    return pl.BlockSpec(shape, lambda i: (0,) * len(shape))


def _proj0(x2d, seq, norm_g, w_bf16, gmat, gains, tables):
    n = x2d.shape[0]
    pos_blocks = seq // PROJ_ROWS
    tab_spec = pl.BlockSpec((PROJ_ROWS, LANES), lambda i: (i % pos_blocks, 0))
    widths = (A_Q_W, A_KV_W, A_KV_W, B_QK_W, B_QK_W, B_V_W, EVEN_MIX_W)
    t_spec = lambda heads: pl.BlockSpec((None, heads, LANES, PROJ_ROWS),
                                        lambda i: (i // pos_blocks, 0, 0, i % pos_blocks))
    t_shape = lambda heads: jax.ShapeDtypeStruct((n // seq, heads, LANES, seq), BF16)
    return pl.pallas_call(
        _proj0_kernel,
        grid=(n // PROJ_ROWS,),
        in_specs=[_row_spec(D_MODEL), _const_spec((1, D_MODEL)), _const_spec(w_bf16.shape),
                  _const_spec((NORM_W, NORM_W))] + [_const_spec((1, LANES))] * 4 + [tab_spec] * 6,
        out_specs=[_row_spec(w) for w in widths] + [t_spec(A_KV_HEADS), t_spec(B_HEADS)],
        out_shape=[jax.ShapeDtypeStruct((n, w), BF16) for w in widths] + [t_shape(A_KV_HEADS), t_shape(B_HEADS)],
        compiler_params=pltpu.CompilerParams(dimension_semantics=("parallel",),
                                             vmem_limit_bytes=VMEM_LIMIT),
        name="proj0",
    )(x2d, norm_g, w_bf16, gmat, *gains, *tables)


def _scores(q, k):
    return lax.dot_general(q, k, (((1,), (1,)), ((), ())), preferred_element_type=F32)


def _lambda(lam_init, lq1_ref, lk1_ref, lq2_ref, lk2_ref):
    return (jnp.exp(jnp.sum(lq1_ref[...] * lk1_ref[...], axis=-1, keepdims=True))
            - jnp.exp(jnp.sum(lq2_ref[...] * lk2_ref[...], axis=-1, keepdims=True)) + lam_init)


def _attn0_small_body(lam_init, lq1_ref, lk1_ref, lq2_ref, lk2_ref, subln_ref,
                      qa_ref, qb_ref, sg_ref, x_ref, ka_ref, vat_ref, kb_ref, vbt_ref, w_ref, y_ref):
    lam = _lambda(lam_init, lq1_ref, lk1_ref, lq2_ref, lk2_ref)

    def exp_scores_t(k, q):
        return jnp.exp(_scores(k, q))

    outs_t = []
    group = A_Q_HEADS // A_KV_HEADS
    for g in range(A_KV_HEADS):
        k = ka_ref[:, g * HEAD_DIM:(g + 1) * HEAD_DIM]
        vt = vat_ref[g]
        es = [exp_scores_t(k, qa_ref[:, (g * group + j) * HEAD_DIM:(g * group + j + 1) * HEAD_DIM]).astype(BF16)
              for j in range(group)]
        for e in es:
            o2 = jnp.dot(vt, e, preferred_element_type=F32)
            outs_t.append(o2[:HEAD_DIM] * (1.0 / o2[HEAD_DIM:]))

    for hb0 in range(0, B_HEADS, B_HEAD_GROUP):
        combined = []
        for hb in range(hb0, hb0 + B_HEAD_GROUP):
            c0 = hb * 2 * HEAD_DIM
            ef = [exp_scores_t(kb_ref[:, c0 + c * HEAD_DIM:c0 + (c + 1) * HEAD_DIM],
                               qb_ref[:, c0 + c * HEAD_DIM:c0 + (c + 1) * HEAD_DIM]) for c in range(2)]
            l0, l1 = [jnp.sum(e, axis=0, keepdims=True) for e in ef]
            a = ef[0].astype(BF16) - ef[1].astype(BF16) * (lam * l0 / l1).astype(BF16)
            combined.append((a, l0))
        for hb, (a, l0) in zip(range(hb0, hb0 + B_HEAD_GROUP), combined):
            outs_t.append(jnp.dot(vbt_ref[hb], a, preferred_element_type=F32) * (1.0 / l0))

    o = jnp.concatenate(outs_t, axis=0).T
    subln = subln_ref[...]
    parts = [o[:, :A_Q_W]]
    for hb in range(B_HEADS):
        c0 = A_Q_W + hb * 2 * HEAD_DIM
        parts.append(_rms_rows(o[:, c0:c0 + 2 * HEAD_DIM], subln) * (1.0 - lam_init))
    mix = (jnp.concatenate(parts, axis=-1) * sg_ref[...].astype(F32)).astype(BF16)
    y_ref[...] = x_ref[...] + jnp.dot(mix, w_ref[...], preferred_element_type=F32)


def _attn0_rowmax_body(lam_init, lq1_ref, lk1_ref, lq2_ref, lk2_ref, subln_ref,
                       qa_ref, qb_ref, sg_ref, x_ref, ka_ref, va_ref, kb_ref, vb_ref, w_ref, y_ref):
    lam = _lambda(lam_init, lq1_ref, lk1_ref, lq2_ref, lk2_ref)
    ones_cols = jnp.ones((ka_ref.shape[0], LANES), BF16)

    def exp_scores(q, k):
        s = _scores(q, k)
        return jnp.exp(s - jnp.max(s, axis=-1, keepdims=True)).astype(BF16)

    outs = []
    group = A_Q_HEADS // A_KV_HEADS
    va_ext = jnp.concatenate([va_ref[...], ones_cols], axis=1)
    for g in range(A_KV_HEADS):
        k = ka_ref[:, g * HEAD_DIM:(g + 1) * HEAD_DIM]
        for j in range(group):
            hq = g * group + j
            o2 = jnp.dot(exp_scores(qa_ref[:, hq * HEAD_DIM:(hq + 1) * HEAD_DIM], k), va_ext,
                         preferred_element_type=F32)
            outs.append(o2[:, g * HEAD_DIM:(g + 1) * HEAD_DIM] * (1.0 / o2[:, LANES:LANES + HEAD_DIM]))

    subln = subln_ref[...]
    for hb in range(B_HEADS):
        c0 = hb * 2 * HEAD_DIM
        vb_ext = jnp.concatenate([vb_ref[:, c0:c0 + 2 * HEAD_DIM], ones_cols], axis=1)
        parts = [jnp.dot(exp_scores(qb_ref[:, c0 + c * HEAD_DIM:c0 + (c + 1) * HEAD_DIM],
                                    kb_ref[:, c0 + c * HEAD_DIM:c0 + (c + 1) * HEAD_DIM]), vb_ext,
                         preferred_element_type=F32) for c in range(2)]
        o = (parts[0][:, :LANES] * (1.0 / parts[0][:, LANES:])
             - parts[1][:, :LANES] * (lam / parts[1][:, LANES:]))
        outs.append(_rms_rows(o, subln) * (1.0 - lam_init))

    mix = (jnp.concatenate(outs, axis=-1) * sg_ref[...].astype(F32)).astype(BF16)
    y_ref[...] = x_ref[...] + jnp.dot(mix, w_ref[...], preferred_element_type=F32)


def _attn0(small, lam_init, x3d, qa, ka, qb, kb, sg, w_out, lams, subln, v_a, v_b):
    b, s, _ = x3d.shape
    rows = ATTN_SCORE_ELEMS // s if small else ATTN_ROWS_ROWMAX
    r3 = lambda t: t.reshape(b, s, t.shape[-1])
    qa, ka, qb, kb, sg = map(r3, (qa, ka, qb, kb, sg))
    row = lambda w: pl.BlockSpec((None, rows, w), lambda bi, i: (bi, i, 0))
    full = lambda w: pl.BlockSpec((None, s, w), lambda bi, i: (bi, 0, 0))
    full_t = lambda heads: pl.BlockSpec((None, heads, LANES, s), lambda bi, i: (bi, 0, 0, 0))
    const = lambda shape: pl.BlockSpec(shape, lambda bi, i: (0,) * len(shape))
    if small:
        body, v_specs = _attn0_small_body, (full_t(A_KV_HEADS), full_t(B_HEADS))
    else:
        body, v_specs = _attn0_rowmax_body, (full(A_KV_W), full(B_V_W))
        v_a, v_b = r3(v_a), r3(v_b)
    return pl.pallas_call(
        functools.partial(body, lam_init),
        grid=(b, s // rows),
        in_specs=[const((1, HEAD_DIM))] * 4 + [const((1, 2 * HEAD_DIM)),
                  row(A_Q_W), row(B_QK_W), row(EVEN_MIX_W), row(D_MODEL),
                  full(A_KV_W), v_specs[0], full(B_QK_W), v_specs[1], const(w_out.shape)],
        out_specs=row(D_MODEL),
        out_shape=jax.ShapeDtypeStruct((b, s, D_MODEL), F32),
        compiler_params=pltpu.CompilerParams(dimension_semantics=("parallel", "arbitrary"),
                                             vmem_limit_bytes=VMEM_LIMIT),
        name="attn0" if small else "attn0_rowmax",
    )(*lams, subln, qa, qb, sg, x3d, ka, v_a, kb, v_b, w_out)


def _proj1_kernel(x_ref, g_ref, w_ref, gmat_ref, cq_ref, ck_ref, prc_ref, prsa_ref, prsb_ref,
                  q_ref, k_ref, v_ref, sg_ref):
    h, row_scale = _scaled_rows(x_ref, g_ref)
    gmat = gmat_ref[...]
    pr = tuple(_wide(t[...]) for t in (prc_ref, prsa_ref, prsb_ref))

    def proj(c0, width):
        return jnp.dot(h, w_ref[:, c0:c0 + width], preferred_element_type=F32) * row_scale

    for out_ref, gain_ref, c0, scale in ((q_ref, cq_ref, 0, QK_SCALE), (k_ref, ck_ref, C_W, 1.0)):
        gain = _wide(gain_ref[...])
        for half_w in range(2):
            z = proj(c0 + half_w * (C_W // 2), C_W // 2)
            for c in range(C_W // 2 // NORM_W):
                sl = slice(c * NORM_W, (c + 1) * NORM_W)
                out_ref[:, half_w * (C_W // 2) + c * NORM_W: half_w * (C_W // 2) + (c + 1) * NORM_W] = (
                    _head_norm_rope(z[:, sl], gmat, gain, *pr, ROPE_DIMS // 2, scale))
    v_ref[...] = proj(2 * C_W, C_W).astype(BF16)
    gate = proj(3 * C_W, C_W)
    sg_ref[...] = (gate * jax.nn.sigmoid(gate)).astype(BF16)


def _proj1(x2d, seq, norm_g, w_bf16, gmat, gains, tables):
    n = x2d.shape[0]
    pos_blocks = seq // PROJ_ROWS
    tab_spec = pl.BlockSpec((PROJ_ROWS, LANES), lambda i: (i % pos_blocks, 0))
    return pl.pallas_call(
        _proj1_kernel,
        grid=(n // PROJ_ROWS,),
        in_specs=[_row_spec(D_MODEL), _const_spec((1, D_MODEL)), _const_spec(w_bf16.shape),
                  _const_spec((NORM_W, NORM_W))] + [_const_spec((1, LANES))] * 2 + [tab_spec] * 3,
        out_specs=[_row_spec(C_W)] * 4,
        out_shape=[jax.ShapeDtypeStruct((n, C_W), BF16)] * 4,
        compiler_params=pltpu.CompilerParams(dimension_semantics=("parallel",),
                                             vmem_limit_bytes=VMEM_LIMIT),
        name="proj1",
    )(x2d, norm_g, w_bf16, gmat, *gains, *tables)


def _dilated_kernel(seq, small, q_ref, k_ref, v_ref, sg_ref, o_ref, qf, kf, vf, qd, kd, vd, bias, acc, den, mx=None):
    qf[...] = q_ref[...].astype(F32)
    kf[...] = k_ref[...].astype(F32)
    vf[...] = v_ref[...].astype(F32)

    rr = lax.broadcasted_iota(jnp.int32, (BAND_ROWS, 2 * BAND_ROWS), 0)
    cc = lax.broadcasted_iota(jnp.int32, (BAND_ROWS, 2 * BAND_ROWS), 1)
    band = (cc >= rr) & (cc <= rr + 2 * BAND_RADIUS)
    not_before = cc >= BAND_RADIUS
    not_after = cc < BAND_ROWS + BAND_RADIUS
    for idx, ok in enumerate((band, band & not_before, band & not_after, band & not_before & not_after)):
        bias[idx] = jnp.where(ok, 0.0, NEG).astype(F32)

    lane = lax.broadcasted_iota(jnp.int32, (1, LANES), 1)
    head0 = lane < HEAD_DIM
    zeros_pad = jnp.zeros((BAND_RADIUS, LANES), BF16)
    ones_cols = jnp.ones((2 * BAND_ROWS, LANES), BF16)

    for p, (_, dil) in enumerate(C_PATTERNS):
        length = seq // dil
        padded = length + 2 * BAND_RADIUS
        blocks = length // BAND_ROWS
        q_src = q_ref if dil == 1 else qd
        for r in range(dil):
            if dil == 1:
                k_rows, v_rows = k_ref[...], v_ref[...]
            else:
                rows = pl.ds(r, length, stride=dil)
                qd[r * length:(r + 1) * length, :] = qf[rows, :].astype(BF16)
                k_rows, v_rows = kf[rows, :].astype(BF16), vf[rows, :].astype(BF16)
            kd[r * padded:r * padded + BAND_RADIUS, :] = zeros_pad
            kd[r * padded + BAND_RADIUS:r * padded + BAND_RADIUS + length, :] = k_rows
            kd[r * padded + BAND_RADIUS + length:(r + 1) * padded, :] = zeros_pad
            vd[r * padded:r * padded + BAND_RADIUS, :] = zeros_pad
            vd[r * padded + BAND_RADIUS:r * padded + BAND_RADIUS + length, :] = v_rows
            vd[r * padded + BAND_RADIUS + length:(r + 1) * padded, :] = zeros_pad

        def block(t, carry, dil=dil, length=length, padded=padded, blocks=blocks, p=p, q_src=q_src):
            r = t // blocks
            i = t - r * blocks
            q0 = pl.multiple_of(r * length + i * BAND_ROWS, BAND_ROWS)
            k0 = pl.multiple_of(r * padded + i * BAND_ROWS, BAND_RADIUS)
            qb = q_src[pl.ds(q0, BAND_ROWS), :]
            kb = kd[pl.ds(k0, 2 * BAND_ROWS), :]
            vb = vd[pl.ds(k0, 2 * BAND_ROWS), :]
            which = jnp.where(i == 0, 1, 0) + jnp.where(i == blocks - 1, 2, 0)
            bias_blk = bias[which]
            zq = jnp.zeros_like(qb)
            qs = jnp.concatenate([jnp.where(head0, qb, zq), jnp.where(head0, zq, qb)], axis=0)
            s = lax.dot_general(qs, kb, (((1,), (1,)), ((), ())), preferred_element_type=F32)
            s = s + jnp.concatenate([bias_blk, bias_blk], axis=0)
            if small:
                e = jnp.exp(s).astype(BF16)
            else:
                m = jnp.max(s, axis=-1, keepdims=True)
                e = jnp.exp(s - m).astype(BF16)
            o2 = jnp.dot(e, jnp.concatenate([vb, ones_cols], axis=1), preferred_element_type=F32)
            o = jnp.where(head0, o2[:BAND_ROWS, :LANES], o2[BAND_ROWS:, :LANES])
            l2 = jnp.where(head0, o2[:BAND_ROWS, LANES:], o2[BAND_ROWS:, LANES:])
            if dil > 1:
                dst = pl.ds(i * (BAND_ROWS * dil) + r, BAND_ROWS, stride=dil)
            else:
                dst = pl.ds(q0, BAND_ROWS)
            acc[p, dst, :] = o
            den[p, dst, :] = l2
            if not small:
                mb = jnp.broadcast_to(m, (2 * BAND_ROWS, LANES))
                mx[p, dst, :] = jnp.where(head0, mb[:BAND_ROWS], mb[BAND_ROWS:])
            return carry

        lax.fori_loop(0, dil * blocks, block, 0, unroll=BLOCK_UNROLL_SMALL if small else BLOCK_UNROLL)

    num = jnp.zeros((seq, LANES), F32)
    dsum = jnp.zeros((seq, LANES), F32)
    if small:
        for p in range(len(C_PATTERNS)):
            num = num + acc[p]
            dsum = dsum + den[p]
    else:
        m_all = jnp.maximum(jnp.maximum(mx[0], mx[1]), mx[2])
        for p in range(len(C_PATTERNS)):
            w = jnp.exp(mx[p] - m_all)
            num = num + w * acc[p]
            dsum = dsum + w * den[p]
    o_ref[...] = (num / dsum * sg_ref[...].astype(F32)).astype(BF16)


def _dilated(small, q, k, v, sg, b, s):
    r3 = lambda t: t.reshape(b, s, C_W)
    spec = pl.BlockSpec((None, s, LANES), lambda bi, hp: (bi, 0, hp))
    n_pat = len(C_PATTERNS)
    max_dil = max(d for _, d in C_PATTERNS)
    padded_rows = s + 2 * BAND_RADIUS * max_dil
    return pl.pallas_call(
        functools.partial(_dilated_kernel, s, small),
        grid=(b, C_W // LANES),
        in_specs=[spec] * 4,
        out_specs=spec,
        out_shape=jax.ShapeDtypeStruct((b, s, C_W), BF16),
        scratch_shapes=[pltpu.VMEM((s, LANES), F32)] * 3
        + [pltpu.VMEM((s, LANES), BF16)]
        + [pltpu.VMEM((padded_rows, LANES), BF16)] * 2
        + [pltpu.VMEM((4, BAND_ROWS, 2 * BAND_ROWS), F32)]
        + [pltpu.VMEM((n_pat, s, LANES), F32)] * (2 if small else 3),
        compiler_params=pltpu.CompilerParams(dimension_semantics=("parallel", "arbitrary"),
                                             vmem_limit_bytes=VMEM_LIMIT),
        name="dilated" if small else "dilated_rowmax",
    )(r3(q), r3(k), r3(v), r3(sg))


def _out1_kernel(mix_ref, x_ref, w_ref, y_ref):
    y_ref[...] = x_ref[...] + jnp.dot(mix_ref[...], w_ref[...], preferred_element_type=F32)


def _out1(mix2d, x2d, w_bf16):
    n = x2d.shape[0]
    return pl.pallas_call(
        _out1_kernel,
        grid=(n // PROJ_ROWS,),
        in_specs=[_row_spec(C_W), _row_spec(D_MODEL), _const_spec(w_bf16.shape)],
        out_specs=_row_spec(D_MODEL),
        out_shape=jax.ShapeDtypeStruct((n, D_MODEL), F32),
        compiler_params=pltpu.CompilerParams(dimension_semantics=("parallel",),
                                             vmem_limit_bytes=VMEM_LIMIT),
        name="out1",
    )(mix2d, x2d, w_bf16)


def _trunk(x, p):
    b, s, _ = x.shape
    x2d = x.reshape(b * s, D_MODEL)
    qa, ka, va, qb, kb, vb, sg, vat, vbt = _proj0(x2d, s, p["norm0"], p["w_in0"], p["gmat"], p["gains0"],
                                                  p["tables"])
    attn0 = lambda small: (lambda *ops: _attn0(small, p["lam_init"], *ops[:-4],
                                               *(ops[-2:] if small else ops[-4:-2])))
    y0 = lax.cond(p["small0"][0] > 0, attn0(True), attn0(False),
                  x, qa, ka, qb, kb, sg, p["w_out0"], p["lams"], p["subln"], va, vb, vat, vbt)
    y0_2d = y0.reshape(b * s, D_MODEL)
    q, k, v, sg1 = _proj1(y0_2d, s, p["norm1"], p["w_in1"], p["gmat"], p["gains1"], p["tables"][3:])
    mix = lax.cond(p["small1"][0] > 0, functools.partial(_dilated, True, b=b, s=s),
                   functools.partial(_dilated, False, b=b, s=s), q, k, v, sg1)
    y1 = _out1(mix.reshape(b * s, C_W), y0_2d, p["w_out1"])
    return y1.reshape(b, s, D_MODEL)


def kernel(x_prompt, x_sample, norm0, w_in0, w_out0, a_q_norm, a_k_norm, b_q_norm, b_k_norm, lambda_q1, lambda_k1, lambda_q2, lambda_k2, b_subln, norm1, w_in1, w_out1, c_q_norm, c_k_norm):
    two = lambda g: jnp.concatenate([g, g]).reshape(1, LANES).astype(F32)
    max_seq = max(x_prompt.shape[1], x_sample.shape[1])
    params = {
        "norm0": norm0.reshape(1, D_MODEL), "norm1": norm1.reshape(1, D_MODEL),
        "w_in0": w_in0.astype(BF16), "w_out0": w_out0.astype(BF16),
        "w_in1": w_in1.astype(BF16), "w_out1": w_out1.astype(BF16),
        "gmat": _head_mean_matrix(),
        "gains0": tuple(two(g) for g in (a_q_norm, a_k_norm, b_q_norm, b_k_norm)),
        "gains1": tuple(two(g) for g in (c_q_norm, c_k_norm)),
        "tables": _rope_tables(max_seq),
        "lams": tuple(t.reshape(1, HEAD_DIM) for t in (lambda_q1, lambda_k1, lambda_q2, lambda_k2)),
        "subln": b_subln.reshape(1, 2 * HEAD_DIM),
        "lam_init": 0.8 - 0.6 * math.exp(-0.3 * 0),
        "small0": _scores_are_small(a_q_norm, a_k_norm) * _scores_are_small(b_q_norm, b_k_norm),
        "small1": _scores_are_small(c_q_norm, c_k_norm),
    }
    return (_trunk(x_prompt, params), _trunk(x_sample, params))
```

```python
import functools
import math

import jax
import jax.numpy as jnp
import numpy as np
from jax import lax
from jax.experimental import pallas as pl
from jax.experimental.pallas import tpu as pltpu

D_MODEL = 1024
HEAD_DIM = 64
GRID_W = 64
EPS = 1e-6
NEG = -1e30

A_Q_HEADS = 8
A_KV_HEADS = 2
AXIAL_THETA = 10000.0
B_HEADS = 4
C_HEADS = 16
C_PATTERNS = ((128, 1), (512, 4), (2048, 16))
ROPE_THETA = 500000.0
ROPE_DIMS = HEAD_DIM // 4

A_Q_W = A_Q_HEADS * HEAD_DIM
A_KV_W = A_KV_HEADS * HEAD_DIM
B_QK_W = B_HEADS * 2 * HEAD_DIM
B_V_W = B_HEADS * 2 * HEAD_DIM
EVEN_MIX_W = A_Q_W + B_V_W
C_W = C_HEADS * HEAD_DIM

LANES = 128
NORM_W = 256
VMEM_LIMIT = 56 * 1024 * 1024
PROJ_ROWS = 1024
ATTN_SCORE_ELEMS = 512 * 2048
ATTN_ROWS_ROWMAX = 256
B_HEAD_GROUP = 2
BAND_ROWS = 128
BAND_RADIUS = 64
BLOCK_UNROLL = 4
BLOCK_UNROLL_SMALL = 32
SAFE_SCORE = 30.0
NORM_SLACK = 1.02
QK_SCALE = HEAD_DIM ** -0.5

BF16 = jnp.bfloat16
F32 = jnp.float32


def _rope_tables(n_pos):
    f32 = np.float32
    pos = np.arange(n_pos, dtype=f32)
    lane = np.arange(HEAD_DIM)

    def angles(p, n_dims, theta):
        freqs = np.power(f32(theta), -np.arange(0, n_dims, 2, dtype=f32) / f32(n_dims)).astype(f32)
        ang = (p[:, None] * freqs[None, :]).astype(f32)
        return np.concatenate([ang, ang], axis=-1)

    half = HEAD_DIM // 2
    row = np.floor(pos / f32(GRID_W)).astype(f32)
    col = pos - row * f32(GRID_W)
    ang = np.concatenate([angles(row, half, AXIAL_THETA), angles(col, half, AXIAL_THETA)], axis=-1)
    first = (lane % half) < (half // 2)
    ax_c = np.cos(ang)
    ax_sa = np.where(first[None, :], -np.sin(ang), 0.0)
    ax_sb = np.where(first[None, :], 0.0, np.sin(ang))

    angp = angles(pos, ROPE_DIMS, ROPE_THETA)
    pad = np.zeros((n_pos, HEAD_DIM - ROPE_DIMS), f32)
    cosp = np.concatenate([np.cos(angp), pad + 1.0], axis=-1)
    sinp = np.concatenate([np.sin(angp), pad], axis=-1)
    p_first = lane < (ROPE_DIMS // 2)
    p_second = (lane >= ROPE_DIMS // 2) & (lane < ROPE_DIMS)
    pr_c = cosp
    pr_sa = np.where(p_first[None, :], -sinp, 0.0)
    pr_sb = np.where(p_second[None, :], sinp, 0.0)

    two = lambda t: np.concatenate([t, t], axis=-1).astype(f32)
    return tuple(two(t) for t in (ax_c, ax_sa, ax_sb, pr_c, pr_sa, pr_sb))


def _scores_are_small(q_gain, k_gain):
    bound = (HEAD_DIM ** 0.5) * NORM_SLACK * jnp.max(jnp.abs(q_gain)) * jnp.max(jnp.abs(k_gain))
    return (bound <= SAFE_SCORE).astype(jnp.int32).reshape(1)


def _head_mean_matrix():
    i = np.arange(NORM_W)
    same = (i[:, None] // HEAD_DIM) == (i[None, :] // HEAD_DIM)
    return jnp.asarray(np.where(same, 1.0 / HEAD_DIM, 0.0), dtype=BF16)


def _rms_rows(x, g):
    return x * lax.rsqrt(jnp.mean(x * x, axis=-1, keepdims=True) + EPS) * g


def _scaled_rows(x_ref, g_ref):
    x = x_ref[...]
    return (x * g_ref[...]).astype(BF16), lax.rsqrt(jnp.mean(x * x, axis=-1, keepdims=True) + EPS)


def _wide(t):
    return jnp.concatenate([t, t], axis=1)


def _head_norm_rope(x, gmat, gain, c, sa, sb, half, scale):
    ss = jnp.dot((x * x).astype(BF16), gmat, preferred_element_type=F32)
    y = x * lax.rsqrt(ss + EPS) * gain
    out = y * c + pltpu.roll(y, NORM_W - half, 1) * sa + pltpu.roll(y, half, 1) * sb
    if scale != 1.0:
        out = out * scale
    return out.astype(BF16)


def _proj0_kernel(x_ref, g_ref, w_ref, gmat_ref, aq_ref, ak_ref, bq_ref, bk_ref,
                  axc_ref, axsa_ref, axsb_ref, prc_ref, prsa_ref, prsb_ref,
                  qa_ref, ka_ref, va_ref, qb_ref, kb_ref, vb_ref, sg_ref, vat_ref, vbt_ref):
    h, row_scale = _scaled_rows(x_ref, g_ref)
    gmat = gmat_ref[...]
    ax = tuple(_wide(t[...]) for t in (axc_ref, axsa_ref, axsb_ref))
    pr = tuple(_wide(t[...]) for t in (prc_ref, prsa_ref, prsb_ref))

    def proj(c0, width):
        return jnp.dot(h, w_ref[:, c0:c0 + width], preferred_element_type=F32) * row_scale

    def normed(z, out_ref, gain, tabs, half, scale):
        for c in range(z.shape[1] // NORM_W):
            sl = slice(c * NORM_W, (c + 1) * NORM_W)
            out_ref[:, sl] = _head_norm_rope(z[:, sl], gmat, _wide(gain), *tabs, half, scale)

    at = dict(zip(("qa", "kv", "qb", "kb", "vb", "gate"),
                  np.cumsum((0, A_Q_W, 2 * A_KV_W, B_QK_W, B_QK_W, B_V_W)).tolist()))
    gate = proj(at["gate"], EVEN_MIX_W)
    sg_ref[...] = (gate * jax.nn.sigmoid(gate)).astype(BF16)
    zb = proj(at["vb"], B_V_W)
    vb_ref[...] = zb.astype(BF16)
    zb_t = zb.T
    for hb in range(B_HEADS):
        vbt_ref[hb] = zb_t[hb * 2 * HEAD_DIM:(hb + 1) * 2 * HEAD_DIM].astype(BF16)
    normed(proj(at["qa"], A_Q_W), qa_ref, aq_ref[...], ax, HEAD_DIM // 4, QK_SCALE)
    zkv = proj(at["kv"], 2 * A_KV_W)
    kv = _head_norm_rope(zkv, gmat, _wide(ak_ref[...]), *ax, HEAD_DIM // 4, 1.0)
    ka_ref[...] = kv[:, :A_KV_W]
    zv = zkv[:, A_KV_W:]
    va_ref[...] = zv.astype(BF16)
    zv_t = zv.T
    ones_rows = jnp.ones((HEAD_DIM, zv_t.shape[1]), F32)
    for g in range(A_KV_HEADS):
        vat_ref[g] = jnp.concatenate([zv_t[g * HEAD_DIM:(g + 1) * HEAD_DIM], ones_rows], axis=0).astype(BF16)
    normed(proj(at["qb"], B_QK_W), qb_ref, bq_ref[...], pr, ROPE_DIMS // 2, QK_SCALE)
    normed(proj(at["kb"], B_QK_W), kb_ref, bk_ref[...], pr, ROPE_DIMS // 2, 1.0)


def _row_spec(width, rows=PROJ_ROWS):
    return pl.BlockSpec((rows, width), lambda i: (i, 0))


def _const_spec(shape):
    return pl.BlockSpec(shape, lambda i: (0,) * len(shape))


def _proj0(x2d, seq, norm_g, w_bf16, gmat, gains, tables):
    n = x2d.shape[0]
    pos_blocks = seq // PROJ_ROWS
    tab_spec = pl.BlockSpec((PROJ_ROWS, LANES), lambda i: (i % pos_blocks, 0))
    widths = (A_Q_W, A_KV_W, A_KV_W, B_QK_W, B_QK_W, B_V_W, EVEN_MIX_W)
    t_spec = lambda heads: pl.BlockSpec((None, heads, LANES, PROJ_ROWS),
                                        lambda i: (i // pos_blocks, 0, 0, i % pos_blocks))
    t_shape = lambda heads: jax.ShapeDtypeStruct((n // seq, heads, LANES, seq), BF16)
    return pl.pallas_call(
        _proj0_kernel,
        grid=(n // PROJ_ROWS,),
        in_specs=[_row_spec(D_MODEL), _const_spec((1, D_MODEL)), _const_spec(w_bf16.shape),
                  _const_spec((NORM_W, NORM_W))] + [_const_spec((1, LANES))] * 4 + [tab_spec] * 6,
        out_specs=[_row_spec(w) for w in widths] + [t_spec(A_KV_HEADS), t_spec(B_HEADS)],
        out_shape=[jax.ShapeDtypeStruct((n, w), BF16) for w in widths] + [t_shape(A_KV_HEADS), t_shape(B_HEADS)],
        compiler_params=pltpu.CompilerParams(dimension_semantics=("parallel",),
                                             vmem_limit_bytes=VMEM_LIMIT),
        name="proj0",
    )(x2d, norm_g, w_bf16, gmat, *gains, *tables)


def _scores(q, k):
    return lax.dot_general(q, k, (((1,), (1,)), ((), ())), preferred_element_type=F32)


def _lambda(lam_init, lq1_ref, lk1_ref, lq2_ref, lk2_ref):
    return (jnp.exp(jnp.sum(lq1_ref[...] * lk1_ref[...], axis=-1, keepdims=True))
            - jnp.exp(jnp.sum(lq2_ref[...] * lk2_ref[...], axis=-1, keepdims=True)) + lam_init)


def _attn0_small_body(lam_init, lq1_ref, lk1_ref, lq2_ref, lk2_ref, subln_ref,
                      qa_ref, qb_ref, sg_ref, x_ref, ka_ref, vat_ref, kb_ref, vbt_ref, w_ref, y_ref):
    lam = _lambda(lam_init, lq1_ref, lk1_ref, lq2_ref, lk2_ref)

    def exp_scores_t(k, q):
        return jnp.exp(_scores(k, q))

    outs_t = []
    group = A_Q_HEADS // A_KV_HEADS
    for g in range(A_KV_HEADS):
        k = ka_ref[:, g * HEAD_DIM:(g + 1) * HEAD_DIM]
        vt = vat_ref[g]
        es = [exp_scores_t(k, qa_ref[:, (g * group + j) * HEAD_DIM:(g * group + j + 1) * HEAD_DIM]).astype(BF16)
              for j in range(group)]
        for e in es:
            o2 = jnp.dot(vt, e, preferred_element_type=F32)
            outs_t.append(o2[:HEAD_DIM] * (1.0 / o2[HEAD_DIM:]))

    for hb0 in range(0, B_HEADS, B_HEAD_GROUP):
        combined = []
        for hb in range(hb0, hb0 + B_HEAD_GROUP):
            c0 = hb * 2 * HEAD_DIM
            ef = [exp_scores_t(kb_ref[:, c0 + c * HEAD_DIM:c0 + (c + 1) * HEAD_DIM],
                               qb_ref[:, c0 + c * HEAD_DIM:c0 + (c + 1) * HEAD_DIM]) for c in range(2)]
            l0, l1 = [jnp.sum(e, axis=0, keepdims=True) for e in ef]
            a = ef[0].astype(BF16) - ef[1].astype(BF16) * (lam * l0 / l1).astype(BF16)
            combined.append((a, l0))
        for hb, (a, l0) in zip(range(hb0, hb0 + B_HEAD_GROUP), combined):
            outs_t.append(jnp.dot(vbt_ref[hb], a, preferred_element_type=F32) * (1.0 / l0))

    o = jnp.concatenate(outs_t, axis=0).T
    subln = subln_ref[...]
    parts = [o[:, :A_Q_W]]
    for hb in range(B_HEADS):
        c0 = A_Q_W + hb * 2 * HEAD_DIM
        parts.append(_rms_rows(o[:, c0:c0 + 2 * HEAD_DIM], subln) * (1.0 - lam_init))
    mix = (jnp.concatenate(parts, axis=-1) * sg_ref[...].astype(F32)).astype(BF16)
    y_ref[...] = x_ref[...] + jnp.dot(mix, w_ref[...], preferred_element_type=F32)


def _attn0_rowmax_body(lam_init, lq1_ref, lk1_ref, lq2_ref, lk2_ref, subln_ref,
                       qa_ref, qb_ref, sg_ref, x_ref, ka_ref, va_ref, kb_ref, vb_ref, w_ref, y_ref):
    lam = _lambda(lam_init, lq1_ref, lk1_ref, lq2_ref, lk2_ref)
    ones_cols = jnp.ones((ka_ref.shape[0], LANES), BF16)

    def exp_scores(q, k):
        s = _scores(q, k)
        return jnp.exp(s - jnp.max(s, axis=-1, keepdims=True)).astype(BF16)

    outs = []
    group = A_Q_HEADS // A_KV_HEADS
    va_ext = jnp.concatenate([va_ref[...], ones_cols], axis=1)
    for g in range(A_KV_HEADS):
        k = ka_ref[:, g * HEAD_DIM:(g + 1) * HEAD_DIM]
        for j in range(group):
            hq = g * group + j
            o2 = jnp.dot(exp_scores(qa_ref[:, hq * HEAD_DIM:(hq + 1) * HEAD_DIM], k), va_ext,
                         preferred_element_type=F32)
            outs.append(o2[:, g * HEAD_DIM:(g + 1) * HEAD_DIM] * (1.0 / o2[:, LANES:LANES + HEAD_DIM]))

    subln = subln_ref[...]
    for hb in range(B_HEADS):
        c0 = hb * 2 * HEAD_DIM
        vb_ext = jnp.concatenate([vb_ref[:, c0:c0 + 2 * HEAD_DIM], ones_cols], axis=1)
        parts = [jnp.dot(exp_scores(qb_ref[:, c0 + c * HEAD_DIM:c0 + (c + 1) * HEAD_DIM],
                                    kb_ref[:, c0 + c * HEAD_DIM:c0 + (c + 1) * HEAD_DIM]), vb_ext,
                         preferred_element_type=F32) for c in range(2)]
        o = (parts[0][:, :LANES] * (1.0 / parts[0][:, LANES:])
             - parts[1][:, :LANES] * (lam / parts[1][:, LANES:]))
        outs.append(_rms_rows(o, subln) * (1.0 - lam_init))

    mix = (jnp.concatenate(outs, axis=-1) * sg_ref[...].astype(F32)).astype(BF16)
    y_ref[...] = x_ref[...] + jnp.dot(mix, w_ref[...], preferred_element_type=F32)


def _attn0(small, lam_init, x3d, qa, ka, qb, kb, sg, w_out, lams, subln, v_a, v_b):
    b, s, _ = x3d.shape
    rows = ATTN_SCORE_ELEMS // s if small else ATTN_ROWS_ROWMAX
    r3 = lambda t: t.reshape(b, s, t.shape[-1])
    qa, ka, qb, kb, sg = map(r3, (qa, ka, qb, kb, sg))
    row = lambda w: pl.BlockSpec((None, rows, w), lambda bi, i: (bi, i, 0))
    full = lambda w: pl.BlockSpec((None, s, w), lambda bi, i: (bi, 0, 0))
    full_t = lambda heads: pl.BlockSpec((None, heads, LANES, s), lambda bi, i: (bi, 0, 0, 0))
    const = lambda shape: pl.BlockSpec(shape, lambda bi, i: (0,) * len(shape))
    if small:
        body, v_specs = _attn0_small_body, (full_t(A_KV_HEADS), full_t(B_HEADS))
    else:
        body, v_specs = _attn0_rowmax_body, (full(A_KV_W), full(B_V_W))
        v_a, v_b = r3(v_a), r3(v_b)
    return pl.pallas_call(
        functools.partial(body, lam_init),
        grid=(b, s // rows),
        in_specs=[const((1, HEAD_DIM))] * 4 + [const((1, 2 * HEAD_DIM)),
                  row(A_Q_W), row(B_QK_W), row(EVEN_MIX_W), row(D_MODEL),
                  full(A_KV_W), v_specs[0], full(B_QK_W), v_specs[1], const(w_out.shape)],
        out_specs=row(D_MODEL),
        out_shape=jax.ShapeDtypeStruct((b, s, D_MODEL), F32),
        compiler_params=pltpu.CompilerParams(dimension_semantics=("parallel", "arbitrary"),
                                             vmem_limit_bytes=VMEM_LIMIT),
        name="attn0" if small else "attn0_rowmax",
    )(*lams, subln, qa, qb, sg, x3d, ka, v_a, kb, v_b, w_out)


def _proj1_kernel(x_ref, g_ref, w_ref, gmat_ref, cq_ref, ck_ref, prc_ref, prsa_ref, prsb_ref,
                  q_ref, k_ref, v_ref, sg_ref):
    h, row_scale = _scaled_rows(x_ref, g_ref)
    gmat = gmat_ref[...]
    pr = tuple(_wide(t[...]) for t in (prc_ref, prsa_ref, prsb_ref))

    def proj(c0, width):
        return jnp.dot(h, w_ref[:, c0:c0 + width], preferred_element_type=F32) * row_scale

    gate = proj(3 * C_W, C_W)
    sg_ref[...] = (gate * jax.nn.sigmoid(gate)).astype(BF16)
    v_ref[...] = proj(2 * C_W, C_W).astype(BF16)
    for out_ref, gain_ref, c0, scale in ((q_ref, cq_ref, 0, QK_SCALE), (k_ref, ck_ref, C_W, 1.0)):
        gain = _wide(gain_ref[...])
        for half_w in range(2):
            z = proj(c0 + half_w * (C_W // 2), C_W // 2)
            for c in range(C_W // 2 // NORM_W):
                sl = slice(c * NORM_W, (c + 1) * NORM_W)
                out_ref[:, half_w * (C_W // 2) + c * NORM_W: half_w * (C_W // 2) + (c + 1) * NORM_W] = (
                    _head_norm_rope(z[:, sl], gmat, gain, *pr, ROPE_DIMS // 2, scale))


def _proj1(x2d, seq, norm_g, w_bf16, gmat, gains, tables):
    n = x2d.shape[0]
    pos_blocks = seq // PROJ_ROWS
    tab_spec = pl.BlockSpec((PROJ_ROWS, LANES), lambda i: (i % pos_blocks, 0))
    return pl.pallas_call(
        _proj1_kernel,
        grid=(n // PROJ_ROWS,),
        in_specs=[_row_spec(D_MODEL), _const_spec((1, D_MODEL)), _const_spec(w_bf16.shape),
                  _const_spec((NORM_W, NORM_W))] + [_const_spec((1, LANES))] * 2 + [tab_spec] * 3,
        out_specs=[_row_spec(C_W)] * 4,
        out_shape=[jax.ShapeDtypeStruct((n, C_W), BF16)] * 4,
        compiler_params=pltpu.CompilerParams(dimension_semantics=("parallel",),
                                             vmem_limit_bytes=VMEM_LIMIT),
        name="proj1",
    )(x2d, norm_g, w_bf16, gmat, *gains, *tables)


def _dilated_kernel(seq, small, q_ref, k_ref, v_ref, sg_ref, o_ref, qf, kf, vf, qd, kd, vd, bias, acc, den, mx=None):
    qf[...] = q_ref[...].astype(F32)
    kf[...] = k_ref[...].astype(F32)
    vf[...] = v_ref[...].astype(F32)

    rr = lax.broadcasted_iota(jnp.int32, (BAND_ROWS, 2 * BAND_ROWS), 0)
    cc = lax.broadcasted_iota(jnp.int32, (BAND_ROWS, 2 * BAND_ROWS), 1)
    band = (cc >= rr) & (cc <= rr + 2 * BAND_RADIUS)
    not_before = cc >= BAND_RADIUS
    not_after = cc < BAND_ROWS + BAND_RADIUS
    for idx, ok in enumerate((band, band & not_before, band & not_after, band & not_before & not_after)):
        bias[idx] = jnp.where(ok, 0.0, NEG).astype(F32)

    lane = lax.broadcasted_iota(jnp.int32, (1, LANES), 1)
    head0 = lane < HEAD_DIM
    zeros_pad = jnp.zeros((BAND_RADIUS, LANES), BF16)
    ones_cols = jnp.ones((2 * BAND_ROWS, LANES), BF16)

    for p, (_, dil) in enumerate(C_PATTERNS):
        length = seq // dil
        padded = length + 2 * BAND_RADIUS
        blocks = length // BAND_ROWS
        q_src = q_ref if dil == 1 else qd
        for r in range(dil):
            if dil == 1:
                k_rows, v_rows = k_ref[...], v_ref[...]
            else:
                rows = pl.ds(r, length, stride=dil)
                qd[r * length:(r + 1) * length, :] = qf[rows, :].astype(BF16)
                k_rows, v_rows = kf[rows, :].astype(BF16), vf[rows, :].astype(BF16)
            kd[r * padded:r * padded + BAND_RADIUS, :] = zeros_pad
            kd[r * padded + BAND_RADIUS:r * padded + BAND_RADIUS + length, :] = k_rows
            kd[r * padded + BAND_RADIUS + length:(r + 1) * padded, :] = zeros_pad
            vd[r * padded:r * padded + BAND_RADIUS, :] = zeros_pad
            vd[r * padded + BAND_RADIUS:r * padded + BAND_RADIUS + length, :] = v_rows
            vd[r * padded + BAND_RADIUS + length:(r + 1) * padded, :] = zeros_pad

        def block(t, carry, dil=dil, length=length, padded=padded, blocks=blocks, p=p, q_src=q_src):
            r = t // blocks
            i = t - r * blocks
            q0 = pl.multiple_of(r * length + i * BAND_ROWS, BAND_ROWS)
            k0 = pl.multiple_of(r * padded + i * BAND_ROWS, BAND_RADIUS)
            qb = q_src[pl.ds(q0, BAND_ROWS), :]
            kb = kd[pl.ds(k0, 2 * BAND_ROWS), :]
            vb = vd[pl.ds(k0, 2 * BAND_ROWS), :]
            which = jnp.where(i == 0, 1, 0) + jnp.where(i == blocks - 1, 2, 0)
            bias_blk = bias[which]
            zq = jnp.zeros_like(qb)
            qs = jnp.concatenate([jnp.where(head0, qb, zq), jnp.where(head0, zq, qb)], axis=0)
            s = lax.dot_general(qs, kb, (((1,), (1,)), ((), ())), preferred_element_type=F32)
            s = s + jnp.concatenate([bias_blk, bias_blk], axis=0)
            if small:
                e = jnp.exp(s).astype(BF16)
            else:
                m = jnp.max(s, axis=-1, keepdims=True)
                e = jnp.exp(s - m).astype(BF16)
            o2 = jnp.dot(e, jnp.concatenate([vb, ones_cols], axis=1), preferred_element_type=F32)
            o = jnp.where(head0, o2[:BAND_ROWS, :LANES], o2[BAND_ROWS:, :LANES])
            l2 = jnp.where(head0, o2[:BAND_ROWS, LANES:], o2[BAND_ROWS:, LANES:])
            if dil > 1:
                dst = pl.ds(i * (BAND_ROWS * dil) + r, BAND_ROWS, stride=dil)
            else:
                dst = pl.ds(q0, BAND_ROWS)
            acc[p, dst, :] = o
            den[p, dst, :] = l2
            if not small:
                mb = jnp.broadcast_to(m, (2 * BAND_ROWS, LANES))
                mx[p, dst, :] = jnp.where(head0, mb[:BAND_ROWS], mb[BAND_ROWS:])
            return carry

        lax.fori_loop(0, dil * blocks, block, 0, unroll=BLOCK_UNROLL_SMALL if small else BLOCK_UNROLL)

    num = jnp.zeros((seq, LANES), F32)
    dsum = jnp.zeros((seq, LANES), F32)
    if small:
        for p in range(len(C_PATTERNS)):
            num = num + acc[p]
            dsum = dsum + den[p]
    else:
        m_all = jnp.maximum(jnp.maximum(mx[0], mx[1]), mx[2])
        for p in range(len(C_PATTERNS)):
            w = jnp.exp(mx[p] - m_all)
            num = num + w * acc[p]
            dsum = dsum + w * den[p]
    o_ref[...] = (num / dsum * sg_ref[...].astype(F32)).astype(BF16)


def _dilated(small, q, k, v, sg, b, s):
    r3 = lambda t: t.reshape(b, s, C_W)
    spec = pl.BlockSpec((None, s, LANES), lambda bi, hp: (bi, 0, hp))
    n_pat = len(C_PATTERNS)
    max_dil = max(d for _, d in C_PATTERNS)
    padded_rows = s + 2 * BAND_RADIUS * max_dil
    return pl.pallas_call(
        functools.partial(_dilated_kernel, s, small),
        grid=(b, C_W // LANES),
        in_specs=[spec] * 4,
        out_specs=spec,
        out_shape=jax.ShapeDtypeStruct((b, s, C_W), BF16),
        scratch_shapes=[pltpu.VMEM((s, LANES), F32)] * 3
        + [pltpu.VMEM((s, LANES), BF16)]
        + [pltpu.VMEM((padded_rows, LANES), BF16)] * 2
        + [pltpu.VMEM((4, BAND_ROWS, 2 * BAND_ROWS), F32)]
        + [pltpu.VMEM((n_pat, s, LANES), F32)] * (2 if small else 3),
        compiler_params=pltpu.CompilerParams(dimension_semantics=("parallel", "arbitrary"),
                                             vmem_limit_bytes=VMEM_LIMIT),
        name="dilated" if small else "dilated_rowmax",
    )(r3(q), r3(k), r3(v), r3(sg))


def _out1_kernel(mix_ref, x_ref, w_ref, y_ref):
    y_ref[...] = x_ref[...] + jnp.dot(mix_ref[...], w_ref[...], preferred_element_type=F32)


def _out1(mix2d, x2d, w_bf16):
    n = x2d.shape[0]
    return pl.pallas_call(
        _out1_kernel,
        grid=(n // PROJ_ROWS,),
        in_specs=[_row_spec(C_W), _row_spec(D_MODEL), _const_spec(w_bf16.shape)],
        out_specs=_row_spec(D_MODEL),
        out_shape=jax.ShapeDtypeStruct((n, D_MODEL), F32),
        compiler_params=pltpu.CompilerParams(dimension_semantics=("parallel",),
                                             vmem_limit_bytes=VMEM_LIMIT),
        name="out1",
    )(mix2d, x2d, w_bf16)


def _trunk(x, p):
    b, s, _ = x.shape
    x2d = x.reshape(b * s, D_MODEL)
    qa, ka, va, qb, kb, vb, sg, vat, vbt = _proj0(x2d, s, p["norm0"], p["w_in0"], p["gmat"], p["gains0"],
                                                  p["tables"])
    attn0 = lambda small: (lambda *ops: _attn0(small, p["lam_init"], *ops[:-4],
                                               *(ops[-2:] if small else ops[-4:-2])))
    y0 = lax.cond(p["small0"][0] > 0, attn0(True), attn0(False),
                  x, qa, ka, qb, kb, sg, p["w_out0"], p["lams"], p["subln"], va, vb, vat, vbt)
    y0_2d = y0.reshape(b * s, D_MODEL)
    q, k, v, sg1 = _proj1(y0_2d, s, p["norm1"], p["w_in1"], p["gmat"], p["gains1"], p["tables"][3:])
    mix = lax.cond(p["small1"][0] > 0, functools.partial(_dilated, True, b=b, s=s),
                   functools.partial(_dilated, False, b=b, s=s), q, k, v, sg1)
    y1 = _out1(mix.reshape(b * s, C_W), y0_2d, p["w_out1"])
    return y1.reshape(b, s, D_MODEL)


def kernel(x_prompt, x_sample, norm0, w_in0, w_out0, a_q_norm, a_k_norm, b_q_norm, b_k_norm, lambda_q1, lambda_k1, lambda_q2, lambda_k2, b_subln, norm1, w_in1, w_out1, c_q_norm, c_k_norm):
    two = lambda g: jnp.concatenate([g, g]).reshape(1, LANES).astype(F32)
    max_seq = max(x_prompt.shape[1], x_sample.shape[1])
    params = {
        "norm0": norm0.reshape(1, D_MODEL), "norm1": norm1.reshape(1, D_MODEL),
        "w_in0": w_in0.astype(BF16), "w_out0": w_out0.astype(BF16),
        "w_in1": w_in1.astype(BF16), "w_out1": w_out1.astype(BF16),
        "gmat": _head_mean_matrix(),
        "gains0": tuple(two(g) for g in (a_q_norm, a_k_norm, b_q_norm, b_k_norm)),
        "gains1": tuple(two(g) for g in (c_q_norm, c_k_norm)),
        "tables": _rope_tables(max_seq),
        "lams": tuple(t.reshape(1, HEAD_DIM) for t in (lambda_q1, lambda_k1, lambda_q2, lambda_k2)),
        "subln": b_subln.reshape(1, 2 * HEAD_DIM),
        "lam_init": 0.8 - 0.6 * math.exp(-0.3 * 0),
        "small0": _scores_are_small(a_q_norm, a_k_norm) * _scores_are_small(b_q_norm, b_k_norm),
        "small1": _scores_are_small(c_q_norm, c_k_norm),
    }
    return (_trunk(x_prompt, params), _trunk(x_sample, params))
```

```python
import functools
import math

import jax
import jax.numpy as jnp
import numpy as np
from jax import lax
from jax.experimental import pallas as pl
from jax.experimental.pallas import tpu as pltpu

D_MODEL = 1024
HEAD_DIM = 64
GRID_W = 64
EPS = 1e-6
NEG = -1e30

A_Q_HEADS = 8
A_KV_HEADS = 2
AXIAL_THETA = 10000.0
B_HEADS = 4
C_HEADS = 16
C_PATTERNS = ((128, 1), (512, 4), (2048, 16))
ROPE_THETA = 500000.0
ROPE_DIMS = HEAD_DIM // 4

A_Q_W = A_Q_HEADS * HEAD_DIM
A_KV_W = A_KV_HEADS * HEAD_DIM
B_QK_W = B_HEADS * 2 * HEAD_DIM
B_V_W = B_HEADS * 2 * HEAD_DIM
EVEN_MIX_W = A_Q_W + B_V_W
C_W = C_HEADS * HEAD_DIM

LANES = 128
NORM_W = 256
VMEM_LIMIT = 56 * 1024 * 1024
PROJ_ROWS = 1024
ATTN_SCORE_ELEMS = 512 * 2048
ATTN_ROWS_ROWMAX = 256
B_HEAD_GROUP = 2
BAND_ROWS = 128
BAND_RADIUS = 64
BLOCK_UNROLL = 4
SAFE_SCORE = 30.0
NORM_SLACK = 1.02
QK_SCALE = HEAD_DIM ** -0.5

BF16 = jnp.bfloat16
F32 = jnp.float32


def _rope_tables(n_pos):
    f32 = np.float32
    pos = np.arange(n_pos, dtype=f32)
    lane = np.arange(HEAD_DIM)

    def angles(p, n_dims, theta):
        freqs = np.power(f32(theta), -np.arange(0, n_dims, 2, dtype=f32) / f32(n_dims)).astype(f32)
        ang = (p[:, None] * freqs[None, :]).astype(f32)
        return np.concatenate([ang, ang], axis=-1)

    half = HEAD_DIM // 2
    row = np.floor(pos / f32(GRID_W)).astype(f32)
    col = pos - row * f32(GRID_W)
    ang = np.concatenate([angles(row, half, AXIAL_THETA), angles(col, half, AXIAL_THETA)], axis=-1)
    first = (lane % half) < (half // 2)
    ax_c = np.cos(ang)
    ax_sa = np.where(first[None, :], -np.sin(ang), 0.0)
    ax_sb = np.where(first[None, :], 0.0, np.sin(ang))

    angp = angles(pos, ROPE_DIMS, ROPE_THETA)
    pad = np.zeros((n_pos, HEAD_DIM - ROPE_DIMS), f32)
    cosp = np.concatenate([np.cos(angp), pad + 1.0], axis=-1)
    sinp = np.concatenate([np.sin(angp), pad], axis=-1)
    p_first = lane < (ROPE_DIMS // 2)
    p_second = (lane >= ROPE_DIMS // 2) & (lane < ROPE_DIMS)
    pr_c = cosp
    pr_sa = np.where(p_first[None, :], -sinp, 0.0)
    pr_sb = np.where(p_second[None, :], sinp, 0.0)

    two = lambda t: np.concatenate([t, t], axis=-1).astype(f32)
    return tuple(two(t) for t in (ax_c, ax_sa, ax_sb, pr_c, pr_sa, pr_sb))


def _scores_are_small(q_gain, k_gain):
    bound = (HEAD_DIM ** 0.5) * NORM_SLACK * jnp.max(jnp.abs(q_gain)) * jnp.max(jnp.abs(k_gain))
    return (bound <= SAFE_SCORE).astype(jnp.int32).reshape(1)


def _head_mean_matrix():
    i = np.arange(NORM_W)
    same = (i[:, None] // HEAD_DIM) == (i[None, :] // HEAD_DIM)
    return jnp.asarray(np.where(same, 1.0 / HEAD_DIM, 0.0), dtype=BF16)


def _rms_rows(x, g):
    return x * lax.rsqrt(jnp.mean(x * x, axis=-1, keepdims=True) + EPS) * g


def _scaled_rows(x_ref, g_ref):
    x = x_ref[...]
    return (x * g_ref[...]).astype(BF16), lax.rsqrt(jnp.mean(x * x, axis=-1, keepdims=True) + EPS)


def _wide(t):
    return jnp.concatenate([t, t], axis=1)


def _head_norm_rope(x, gmat, gain, c, sa, sb, half, scale):
    ss = jnp.dot((x * x).astype(BF16), gmat, preferred_element_type=F32)
    y = x * lax.rsqrt(ss + EPS) * gain
    out = y * c + pltpu.roll(y, NORM_W - half, 1) * sa + pltpu.roll(y, half, 1) * sb
    if scale != 1.0:
        out = out * scale
    return out.astype(BF16)


def _proj0_kernel(x_ref, g_ref, w_ref, gmat_ref, aq_ref, ak_ref, bq_ref, bk_ref,
                  axc_ref, axsa_ref, axsb_ref, prc_ref, prsa_ref, prsb_ref,
                  qa_ref, ka_ref, va_ref, qb_ref, kb_ref, vb_ref, sg_ref, vat_ref, vbt_ref):
    h, row_scale = _scaled_rows(x_ref, g_ref)
    gmat = gmat_ref[...]
    ax = tuple(_wide(t[...]) for t in (axc_ref, axsa_ref, axsb_ref))
    pr = tuple(_wide(t[...]) for t in (prc_ref, prsa_ref, prsb_ref))

    def proj(c0, width):
        return jnp.dot(h, w_ref[:, c0:c0 + width], preferred_element_type=F32) * row_scale

    def normed(z, out_ref, gain, tabs, half, scale):
        for c in range(z.shape[1] // NORM_W):
            sl = slice(c * NORM_W, (c + 1) * NORM_W)
            out_ref[:, sl] = _head_norm_rope(z[:, sl], gmat, _wide(gain), *tabs, half, scale)

    c0 = 0
    normed(proj(c0, A_Q_W), qa_ref, aq_ref[...], ax, HEAD_DIM // 4, QK_SCALE)
    c0 += A_Q_W
    zkv = proj(c0, 2 * A_KV_W)
    kv = _head_norm_rope(zkv, gmat, _wide(ak_ref[...]), *ax, HEAD_DIM // 4, 1.0)
    ka_ref[...] = kv[:, :A_KV_W]
    zv = zkv[:, A_KV_W:]
    va_ref[...] = zv.astype(BF16)
    zv_t = zv.T
    ones_rows = jnp.ones((HEAD_DIM, zv_t.shape[1]), F32)
    for g in range(A_KV_HEADS):
        vat_ref[g] = jnp.concatenate([zv_t[g * HEAD_DIM:(g + 1) * HEAD_DIM], ones_rows], axis=0).astype(BF16)
    c0 += 2 * A_KV_W
    normed(proj(c0, B_QK_W), qb_ref, bq_ref[...], pr, ROPE_DIMS // 2, QK_SCALE)
    c0 += B_QK_W
    normed(proj(c0, B_QK_W), kb_ref, bk_ref[...], pr, ROPE_DIMS // 2, 1.0)
    c0 += B_QK_W
    zb = proj(c0, B_V_W)
    vb_ref[...] = zb.astype(BF16)
    zb_t = zb.T
    for hb in range(B_HEADS):
        vbt_ref[hb] = zb_t[hb * 2 * HEAD_DIM:(hb + 1) * 2 * HEAD_DIM].astype(BF16)
    c0 += B_V_W
    gate = proj(c0, EVEN_MIX_W)
    sg_ref[...] = (gate * jax.nn.sigmoid(gate)).astype(BF16)


def _row_spec(width, rows=PROJ_ROWS):
    return pl.BlockSpec((rows, width), lambda i: (i, 0))


def _const_spec(shape):
    return pl.BlockSpec(shape, lambda i: (0,) * len(shape))


def _proj0(x2d, seq, norm_g, w_bf16, gmat, gains, tables):
    n = x2d.shape[0]
    pos_blocks = seq // PROJ_ROWS
    tab_spec = pl.BlockSpec((PROJ_ROWS, LANES), lambda i: (i % pos_blocks, 0))
    widths = (A_Q_W, A_KV_W, A_KV_W, B_QK_W, B_QK_W, B_V_W, EVEN_MIX_W)
    t_spec = lambda heads: pl.BlockSpec((None, heads, LANES, PROJ_ROWS),
                                        lambda i: (i // pos_blocks, 0, 0, i % pos_blocks))
    t_shape = lambda heads: jax.ShapeDtypeStruct((n // seq, heads, LANES, seq), BF16)
    return pl.pallas_call(
        _proj0_kernel,
        grid=(n // PROJ_ROWS,),
        in_specs=[_row_spec(D_MODEL), _const_spec((1, D_MODEL)), _const_spec(w_bf16.shape),
                  _const_spec((NORM_W, NORM_W))] + [_const_spec((1, LANES))] * 4 + [tab_spec] * 6,
        out_specs=[_row_spec(w) for w in widths] + [t_spec(A_KV_HEADS), t_spec(B_HEADS)],
        out_shape=[jax.ShapeDtypeStruct((n, w), BF16) for w in widths] + [t_shape(A_KV_HEADS), t_shape(B_HEADS)],
        compiler_params=pltpu.CompilerParams(dimension_semantics=("parallel",),
                                             vmem_limit_bytes=VMEM_LIMIT),
        name="proj0",
    )(x2d, norm_g, w_bf16, gmat, *gains, *tables)


def _scores(q, k):
    return lax.dot_general(q, k, (((1,), (1,)), ((), ())), preferred_element_type=F32)


def _lambda(lam_init, lq1_ref, lk1_ref, lq2_ref, lk2_ref):
    return (jnp.exp(jnp.sum(lq1_ref[...] * lk1_ref[...], axis=-1, keepdims=True))
            - jnp.exp(jnp.sum(lq2_ref[...] * lk2_ref[...], axis=-1, keepdims=True)) + lam_init)


def _attn0_small_body(lam_init, lq1_ref, lk1_ref, lq2_ref, lk2_ref, subln_ref,
                      qa_ref, qb_ref, sg_ref, x_ref, ka_ref, vat_ref, kb_ref, vbt_ref, w_ref, y_ref):
    lam = _lambda(lam_init, lq1_ref, lk1_ref, lq2_ref, lk2_ref)

    def exp_scores_t(k, q):
        return jnp.exp(_scores(k, q))

    outs_t = []
    group = A_Q_HEADS // A_KV_HEADS
    for g in range(A_KV_HEADS):
        k = ka_ref[:, g * HEAD_DIM:(g + 1) * HEAD_DIM]
        vt = vat_ref[g]
        es = [exp_scores_t(k, qa_ref[:, (g * group + j) * HEAD_DIM:(g * group + j + 1) * HEAD_DIM]).astype(BF16)
              for j in range(group)]
        for e in es:
            o2 = jnp.dot(vt, e, preferred_element_type=F32)
            outs_t.append(o2[:HEAD_DIM] * (1.0 / o2[HEAD_DIM:]))

    for hb0 in range(0, B_HEADS, B_HEAD_GROUP):
        combined = []
        for hb in range(hb0, hb0 + B_HEAD_GROUP):
            c0 = hb * 2 * HEAD_DIM
            ef = [exp_scores_t(kb_ref[:, c0 + c * HEAD_DIM:c0 + (c + 1) * HEAD_DIM],
                               qb_ref[:, c0 + c * HEAD_DIM:c0 + (c + 1) * HEAD_DIM]) for c in range(2)]
            l0, l1 = [jnp.sum(e, axis=0, keepdims=True) for e in ef]
            a = ef[0].astype(BF16) - ef[1].astype(BF16) * (lam * l0 / l1).astype(BF16)
            combined.append((a, l0))
        for hb, (a, l0) in zip(range(hb0, hb0 + B_HEAD_GROUP), combined):
            outs_t.append(jnp.dot(vbt_ref[hb], a, preferred_element_type=F32) * (1.0 / l0))

    o = jnp.concatenate(outs_t, axis=0).T
    subln = subln_ref[...]
    parts = [o[:, :A_Q_W]]
    for hb in range(B_HEADS):
        c0 = A_Q_W + hb * 2 * HEAD_DIM
        parts.append(_rms_rows(o[:, c0:c0 + 2 * HEAD_DIM], subln) * (1.0 - lam_init))
    mix = (jnp.concatenate(parts, axis=-1) * sg_ref[...].astype(F32)).astype(BF16)
    y_ref[...] = x_ref[...] + jnp.dot(mix, w_ref[...], preferred_element_type=F32)


def _attn0_rowmax_body(lam_init, lq1_ref, lk1_ref, lq2_ref, lk2_ref, subln_ref,
                       qa_ref, qb_ref, sg_ref, x_ref, ka_ref, va_ref, kb_ref, vb_ref, w_ref, y_ref):
    lam = _lambda(lam_init, lq1_ref, lk1_ref, lq2_ref, lk2_ref)
    ones_cols = jnp.ones((ka_ref.shape[0], LANES), BF16)

    def exp_scores(q, k):
        s = _scores(q, k)
        return jnp.exp(s - jnp.max(s, axis=-1, keepdims=True)).astype(BF16)

    outs = []
    group = A_Q_HEADS // A_KV_HEADS
    va_ext = jnp.concatenate([va_ref[...], ones_cols], axis=1)
    for g in range(A_KV_HEADS):
        k = ka_ref[:, g * HEAD_DIM:(g + 1) * HEAD_DIM]
        for j in range(group):
            hq = g * group + j
            o2 = jnp.dot(exp_scores(qa_ref[:, hq * HEAD_DIM:(hq + 1) * HEAD_DIM], k), va_ext,
                         preferred_element_type=F32)
            outs.append(o2[:, g * HEAD_DIM:(g + 1) * HEAD_DIM] * (1.0 / o2[:, LANES:LANES + HEAD_DIM]))

    subln = subln_ref[...]
    for hb in range(B_HEADS):
        c0 = hb * 2 * HEAD_DIM
        vb_ext = jnp.concatenate([vb_ref[:, c0:c0 + 2 * HEAD_DIM], ones_cols], axis=1)
        parts = [jnp.dot(exp_scores(qb_ref[:, c0 + c * HEAD_DIM:c0 + (c + 1) * HEAD_DIM],
                                    kb_ref[:, c0 + c * HEAD_DIM:c0 + (c + 1) * HEAD_DIM]), vb_ext,
                         preferred_element_type=F32) for c in range(2)]
        o = (parts[0][:, :LANES] * (1.0 / parts[0][:, LANES:])
             - parts[1][:, :LANES] * (lam / parts[1][:, LANES:]))
        outs.append(_rms_rows(o, subln) * (1.0 - lam_init))

    mix = (jnp.concatenate(outs, axis=-1) * sg_ref[...].astype(F32)).astype(BF16)
    y_ref[...] = x_ref[...] + jnp.dot(mix, w_ref[...], preferred_element_type=F32)


def _attn0(small, lam_init, x3d, qa, ka, qb, kb, sg, w_out, lams, subln, v_a, v_b):
    b, s, _ = x3d.shape
    rows = ATTN_SCORE_ELEMS // s if small else ATTN_ROWS_ROWMAX
    r3 = lambda t: t.reshape(b, s, t.shape[-1])
    qa, ka, qb, kb, sg = map(r3, (qa, ka, qb, kb, sg))
    row = lambda w: pl.BlockSpec((None, rows, w), lambda bi, i: (bi, i, 0))
    full = lambda w: pl.BlockSpec((None, s, w), lambda bi, i: (bi, 0, 0))
    full_t = lambda heads: pl.BlockSpec((None, heads, LANES, s), lambda bi, i: (bi, 0, 0, 0))
    const = lambda shape: pl.BlockSpec(shape, lambda bi, i: (0,) * len(shape))
    if small:
        body, v_specs = _attn0_small_body, (full_t(A_KV_HEADS), full_t(B_HEADS))
    else:
        body, v_specs = _attn0_rowmax_body, (full(A_KV_W), full(B_V_W))
        v_a, v_b = r3(v_a), r3(v_b)
    return pl.pallas_call(
        functools.partial(body, lam_init),
        grid=(b, s // rows),
        in_specs=[const((1, HEAD_DIM))] * 4 + [const((1, 2 * HEAD_DIM)),
                  row(A_Q_W), row(B_QK_W), row(EVEN_MIX_W), row(D_MODEL),
                  full(A_KV_W), v_specs[0], full(B_QK_W), v_specs[1], const(w_out.shape)],
        out_specs=row(D_MODEL),
        out_shape=jax.ShapeDtypeStruct((b, s, D_MODEL), F32),
        compiler_params=pltpu.CompilerParams(dimension_semantics=("parallel", "arbitrary"),
                                             vmem_limit_bytes=VMEM_LIMIT),
        name="attn0" if small else "attn0_rowmax",
    )(*lams, subln, qa, qb, sg, x3d, ka, v_a, kb, v_b, w_out)


def _proj1_kernel(x_ref, g_ref, w_ref, gmat_ref, cq_ref, ck_ref, prc_ref, prsa_ref, prsb_ref,
                  q_ref, k_ref, v_ref, sg_ref):
    h, row_scale = _scaled_rows(x_ref, g_ref)
    gmat = gmat_ref[...]
    pr = tuple(_wide(t[...]) for t in (prc_ref, prsa_ref, prsb_ref))

    def proj(c0, width):
        return jnp.dot(h, w_ref[:, c0:c0 + width], preferred_element_type=F32) * row_scale

    for out_ref, gain_ref, c0, scale in ((q_ref, cq_ref, 0, QK_SCALE), (k_ref, ck_ref, C_W, 1.0)):
        gain = _wide(gain_ref[...])
        for half_w in range(2):
            z = proj(c0 + half_w * (C_W // 2), C_W // 2)
            for c in range(C_W // 2 // NORM_W):
                sl = slice(c * NORM_W, (c + 1) * NORM_W)
                out_ref[:, half_w * (C_W // 2) + c * NORM_W: half_w * (C_W // 2) + (c + 1) * NORM_W] = (
                    _head_norm_rope(z[:, sl], gmat, gain, *pr, ROPE_DIMS // 2, scale))
    v_ref[...] = proj(2 * C_W, C_W).astype(BF16)
    gate = proj(3 * C_W, C_W)
    sg_ref[...] = (gate * jax.nn.sigmoid(gate)).astype(BF16)


def _proj1(x2d, seq, norm_g, w_bf16, gmat, gains, tables):
    n = x2d.shape[0]
    pos_blocks = seq // PROJ_ROWS
    tab_spec = pl.BlockSpec((PROJ_ROWS, LANES), lambda i: (i % pos_blocks, 0))
    return pl.pallas_call(
        _proj1_kernel,
        grid=(n // PROJ_ROWS,),
        in_specs=[_row_spec(D_MODEL), _const_spec((1, D_MODEL)), _const_spec(w_bf16.shape),
                  _const_spec((NORM_W, NORM_W))] + [_const_spec((1, LANES))] * 2 + [tab_spec] * 3,
        out_specs=[_row_spec(C_W)] * 4,
        out_shape=[jax.ShapeDtypeStruct((n, C_W), BF16)] * 4,
        compiler_params=pltpu.CompilerParams(dimension_semantics=("parallel",),
                                             vmem_limit_bytes=VMEM_LIMIT),
        name="proj1",
    )(x2d, norm_g, w_bf16, gmat, *gains, *tables)


def _dilated_kernel(seq, small, q_ref, k_ref, v_ref, sg_ref, o_ref, qf, kf, vf, *scratch):
    n_pat = len(C_PATTERNS)
    n_sets = n_pat if small else 1
    sets = [scratch[3 * i:3 * i + 3] for i in range(n_sets)]
    bias, acc, den = scratch[3 * n_sets:3 * n_sets + 3]
    mx = None if small else scratch[3 * n_sets + 3]
    qf[...] = q_ref[...].astype(F32)
    kf[...] = k_ref[...].astype(F32)
    vf[...] = v_ref[...].astype(F32)

    rr = lax.broadcasted_iota(jnp.int32, (BAND_ROWS, 2 * BAND_ROWS), 0)
    cc = lax.broadcasted_iota(jnp.int32, (BAND_ROWS, 2 * BAND_ROWS), 1)
    band = (cc >= rr) & (cc <= rr + 2 * BAND_RADIUS)
    not_before = cc >= BAND_RADIUS
    not_after = cc < BAND_ROWS + BAND_RADIUS
    for idx, ok in enumerate((band, band & not_before, band & not_after, band & not_before & not_after)):
        bias[idx] = jnp.where(ok, 0.0, NEG).astype(F32)

    lane = lax.broadcasted_iota(jnp.int32, (1, LANES), 1)
    head0 = lane < HEAD_DIM
    zeros_pad = jnp.zeros((BAND_RADIUS, LANES), BF16)
    ones_cols = jnp.ones((2 * BAND_ROWS, LANES), BF16)

    def deinterleave(p):
        dil = C_PATTERNS[p][1]
        qd, kd, vd = sets[p % n_sets]
        length = seq // dil
        padded = length + 2 * BAND_RADIUS
        for r in range(dil):
            if dil == 1:
                k_rows, v_rows = k_ref[...], v_ref[...]
            else:
                rows = pl.ds(r, length, stride=dil)
                qd[r * length:(r + 1) * length, :] = qf[rows, :].astype(BF16)
                k_rows, v_rows = kf[rows, :].astype(BF16), vf[rows, :].astype(BF16)
            kd[r * padded:r * padded + BAND_RADIUS, :] = zeros_pad
            kd[r * padded + BAND_RADIUS:r * padded + BAND_RADIUS + length, :] = k_rows
            kd[r * padded + BAND_RADIUS + length:(r + 1) * padded, :] = zeros_pad
            vd[r * padded:r * padded + BAND_RADIUS, :] = zeros_pad
            vd[r * padded + BAND_RADIUS:r * padded + BAND_RADIUS + length, :] = v_rows
            vd[r * padded + BAND_RADIUS + length:(r + 1) * padded, :] = zeros_pad

    def block(p, t):
        dil = C_PATTERNS[p][1]
        qd, kd, vd = sets[p % n_sets]
        length = seq // dil
        padded = length + 2 * BAND_RADIUS
        blocks = length // BAND_ROWS
        q_src = q_ref if dil == 1 else qd
        r = t // blocks
        i = t - r * blocks
        q0, k0 = r * length + i * BAND_ROWS, r * padded + i * BAND_ROWS
        if isinstance(t, int):
            which = (1 if i == 0 else 0) + (2 if i == blocks - 1 else 0)
        else:
            q0, k0 = pl.multiple_of(q0, BAND_ROWS), pl.multiple_of(k0, BAND_RADIUS)
            which = jnp.where(i == 0, 1, 0) + jnp.where(i == blocks - 1, 2, 0)
        qb = q_src[pl.ds(q0, BAND_ROWS), :]
        kb = kd[pl.ds(k0, 2 * BAND_ROWS), :]
        vb = vd[pl.ds(k0, 2 * BAND_ROWS), :]
        bias_blk = bias[which]
        zq = jnp.zeros_like(qb)
        qs = jnp.concatenate([jnp.where(head0, qb, zq), jnp.where(head0, zq, qb)], axis=0)
        s = lax.dot_general(qs, kb, (((1,), (1,)), ((), ())), preferred_element_type=F32)
        s = s + jnp.concatenate([bias_blk, bias_blk], axis=0)
        if small:
            e = jnp.exp(s).astype(BF16)
        else:
            m = jnp.max(s, axis=-1, keepdims=True)
            e = jnp.exp(s - m).astype(BF16)
        o2 = jnp.dot(e, jnp.concatenate([vb, ones_cols], axis=1), preferred_element_type=F32)
        o = jnp.where(head0, o2[:BAND_ROWS, :LANES], o2[BAND_ROWS:, :LANES])
        l2 = jnp.where(head0, o2[:BAND_ROWS, LANES:], o2[BAND_ROWS:, LANES:])
        if dil > 1:
            dst = pl.ds(i * (BAND_ROWS * dil) + r, BAND_ROWS, stride=dil)
        else:
            dst = pl.ds(q0, BAND_ROWS)
        acc[p, dst, :] = o
        den[p, dst, :] = l2
        if not small:
            mb = jnp.broadcast_to(m, (2 * BAND_ROWS, LANES))
            mx[p, dst, :] = jnp.where(head0, mb[:BAND_ROWS], mb[BAND_ROWS:])

    n_blocks = seq // BAND_ROWS
    if small:
        order = sorted(range(n_pat), key=lambda p: -C_PATTERNS[p][1])
        for p in order:
            deinterleave(p)
        for p in order:
            for t in range(n_blocks):
                block(p, t)
    else:
        for p in range(n_pat):
            deinterleave(p)

            def body(t, carry, p=p):
                block(p, t)
                return carry

            lax.fori_loop(0, n_blocks, body, 0, unroll=BLOCK_UNROLL)

    num = jnp.zeros((seq, LANES), F32)
    dsum = jnp.zeros((seq, LANES), F32)
    if small:
        for p in range(len(C_PATTERNS)):
            num = num + acc[p]
            dsum = dsum + den[p]
    else:
        m_all = jnp.maximum(jnp.maximum(mx[0], mx[1]), mx[2])
        for p in range(len(C_PATTERNS)):
            w = jnp.exp(mx[p] - m_all)
            num = num + w * acc[p]
            dsum = dsum + w * den[p]
    o_ref[...] = (num / dsum * sg_ref[...].astype(F32)).astype(BF16)


def _dilated(small, q, k, v, sg, b, s):
    r3 = lambda t: t.reshape(b, s, C_W)
    spec = pl.BlockSpec((None, s, LANES), lambda bi, hp: (bi, 0, hp))
    n_pat = len(C_PATTERNS)
    padded_rows = lambda dil: s + 2 * BAND_RADIUS * dil
    set_dils = [d for _, d in C_PATTERNS] if small else [max(d for _, d in C_PATTERNS)]
    sets = [buf for d in set_dils for buf in (pltpu.VMEM((s, LANES), BF16),
                                               pltpu.VMEM((padded_rows(d), LANES), BF16),
                                               pltpu.VMEM((padded_rows(d), LANES), BF16))]
    return pl.pallas_call(
        functools.partial(_dilated_kernel, s, small),
        grid=(b, C_W // LANES),
        in_specs=[spec] * 4,
        out_specs=spec,
        out_shape=jax.ShapeDtypeStruct((b, s, C_W), BF16),
        scratch_shapes=[pltpu.VMEM((s, LANES), F32)] * 3
        + sets
        + [pltpu.VMEM((4, BAND_ROWS, 2 * BAND_ROWS), F32)]
        + [pltpu.VMEM((n_pat, s, LANES), F32)] * (2 if small else 3),
        compiler_params=pltpu.CompilerParams(dimension_semantics=("parallel", "arbitrary"),
                                             vmem_limit_bytes=VMEM_LIMIT),
        name="dilated" if small else "dilated_rowmax",
    )(r3(q), r3(k), r3(v), r3(sg))


def _out1_kernel(mix_ref, x_ref, w_ref, y_ref):
    y_ref[...] = x_ref[...] + jnp.dot(mix_ref[...], w_ref[...], preferred_element_type=F32)


def _out1(mix2d, x2d, w_bf16):
    n = x2d.shape[0]
    return pl.pallas_call(
        _out1_kernel,
        grid=(n // PROJ_ROWS,),
        in_specs=[_row_spec(C_W), _row_spec(D_MODEL), _const_spec(w_bf16.shape)],
        out_specs=_row_spec(D_MODEL),
        out_shape=jax.ShapeDtypeStruct((n, D_MODEL), F32),
        compiler_params=pltpu.CompilerParams(dimension_semantics=("parallel",),
                                             vmem_limit_bytes=VMEM_LIMIT),
        name="out1",
    )(mix2d, x2d, w_bf16)


def _trunk(x, p):
    b, s, _ = x.shape
    x2d = x.reshape(b * s, D_MODEL)
    qa, ka, va, qb, kb, vb, sg, vat, vbt = _proj0(x2d, s, p["norm0"], p["w_in0"], p["gmat"], p["gains0"],
                                                  p["tables"])
    attn0 = lambda small: (lambda *ops: _attn0(small, p["lam_init"], *ops[:-4],
                                               *(ops[-2:] if small else ops[-4:-2])))
    y0 = lax.cond(p["small0"][0] > 0, attn0(True), attn0(False),
                  x, qa, ka, qb, kb, sg, p["w_out0"], p["lams"], p["subln"], va, vb, vat, vbt)
    y0_2d = y0.reshape(b * s, D_MODEL)
    q, k, v, sg1 = _proj1(y0_2d, s, p["norm1"], p["w_in1"], p["gmat"], p["gains1"], p["tables"][3:])
    mix = lax.cond(p["small1"][0] > 0, functools.partial(_dilated, True, b=b, s=s),
                   functools.partial(_dilated, False, b=b, s=s), q, k, v, sg1)
    y1 = _out1(mix.reshape(b * s, C_W), y0_2d, p["w_out1"])
    return y1.reshape(b, s, D_MODEL)


def kernel(x_prompt, x_sample, norm0, w_in0, w_out0, a_q_norm, a_k_norm, b_q_norm, b_k_norm, lambda_q1, lambda_k1, lambda_q2, lambda_k2, b_subln, norm1, w_in1, w_out1, c_q_norm, c_k_norm):
    two = lambda g: jnp.concatenate([g, g]).reshape(1, LANES).astype(F32)
    max_seq = max(x_prompt.shape[1], x_sample.shape[1])
    params = {
        "norm0": norm0.reshape(1, D_MODEL), "norm1": norm1.reshape(1, D_MODEL),
        "w_in0": w_in0.astype(BF16), "w_out0": w_out0.astype(BF16),
        "w_in1": w_in1.astype(BF16), "w_out1": w_out1.astype(BF16),
        "gmat": _head_mean_matrix(),
        "gains0": tuple(two(g) for g in (a_q_norm, a_k_norm, b_q_norm, b_k_norm)),
        "gains1": tuple(two(g) for g in (c_q_norm, c_k_norm)),
        "tables": _rope_tables(max_seq),
        "lams": tuple(t.reshape(1, HEAD_DIM) for t in (lambda_q1, lambda_k1, lambda_q2, lambda_k2)),
        "subln": b_subln.reshape(1, 2 * HEAD_DIM),
        "lam_init": 0.8 - 0.6 * math.exp(-0.3 * 0),
        "small0": _scores_are_small(a_q_norm, a_k_norm) * _scores_are_small(b_q_norm, b_k_norm),
        "small1": _scores_are_small(c_q_norm, c_k_norm),
    }
    return (_trunk(x_prompt, params), _trunk(x_sample, params))
```

```python
import functools
import math

import jax
import jax.numpy as jnp
import numpy as np
from jax import lax
from jax.experimental import pallas as pl
from jax.experimental.pallas import tpu as pltpu

D_MODEL = 1024
HEAD_DIM = 64
GRID_W = 64
EPS = 1e-6
NEG = -1e30

A_Q_HEADS = 8
A_KV_HEADS = 2
AXIAL_THETA = 10000.0
B_HEADS = 4
C_HEADS = 16
C_PATTERNS = ((128, 1), (512, 4), (2048, 16))
ROPE_THETA = 500000.0
ROPE_DIMS = HEAD_DIM // 4

A_Q_W = A_Q_HEADS * HEAD_DIM
A_KV_W = A_KV_HEADS * HEAD_DIM
B_QK_W = B_HEADS * 2 * HEAD_DIM
B_V_W = B_HEADS * 2 * HEAD_DIM
EVEN_MIX_W = A_Q_W + B_V_W
C_W = C_HEADS * HEAD_DIM

LANES = 128
NORM_W = 256
VMEM_LIMIT = 56 * 1024 * 1024
PROJ_ROWS = 1024
OUT1_INPUT_BUFFERS = 3
ATTN_SCORE_ELEMS = 512 * 2048
ATTN_ROWS_ROWMAX = 256
B_HEAD_GROUP = 2
BAND_ROWS = 128
BAND_RADIUS = 64
BLOCK_UNROLL = 4
SAFE_SCORE = 30.0
NORM_SLACK = 1.02
QK_SCALE = HEAD_DIM ** -0.5

BF16 = jnp.bfloat16
F32 = jnp.float32


def _rope_tables(n_pos):
    f32 = np.float32
    pos = np.arange(n_pos, dtype=f32)
    lane = np.arange(HEAD_DIM)

    def angles(p, n_dims, theta):
        freqs = np.power(f32(theta), -np.arange(0, n_dims, 2, dtype=f32) / f32(n_dims)).astype(f32)
        ang = (p[:, None] * freqs[None, :]).astype(f32)
        return np.concatenate([ang, ang], axis=-1)

    half = HEAD_DIM // 2
    row = np.floor(pos / f32(GRID_W)).astype(f32)
    col = pos - row * f32(GRID_W)
    ang = np.concatenate([angles(row, half, AXIAL_THETA), angles(col, half, AXIAL_THETA)], axis=-1)
    first = (lane % half) < (half // 2)
    ax_c = np.cos(ang)
    ax_sa = np.where(first[None, :], -np.sin(ang), 0.0)
    ax_sb = np.where(first[None, :], 0.0, np.sin(ang))

    angp = angles(pos, ROPE_DIMS, ROPE_THETA)
    pad = np.zeros((n_pos, HEAD_DIM - ROPE_DIMS), f32)
    cosp = np.concatenate([np.cos(angp), pad + 1.0], axis=-1)
    sinp = np.concatenate([np.sin(angp), pad], axis=-1)
    p_first = lane < (ROPE_DIMS // 2)
    p_second = (lane >= ROPE_DIMS // 2) & (lane < ROPE_DIMS)
    pr_c = cosp
    pr_sa = np.where(p_first[None, :], -sinp, 0.0)
    pr_sb = np.where(p_second[None, :], sinp, 0.0)

    two = lambda t: np.concatenate([t, t], axis=-1).astype(f32)
    return tuple(two(t) for t in (ax_c, ax_sa, ax_sb, pr_c, pr_sa, pr_sb))


def _scores_are_small(q_gain, k_gain):
    bound = (HEAD_DIM ** 0.5) * NORM_SLACK * jnp.max(jnp.abs(q_gain)) * jnp.max(jnp.abs(k_gain))
    return (bound <= SAFE_SCORE).astype(jnp.int32).reshape(1)


def _head_mean_matrix():
    i = np.arange(NORM_W)
    same = (i[:, None] // HEAD_DIM) == (i[None, :] // HEAD_DIM)
    return jnp.asarray(np.where(same, 1.0 / HEAD_DIM, 0.0), dtype=BF16)


def _rms_rows(x, g):
    return x * lax.rsqrt(jnp.mean(x * x, axis=-1, keepdims=True) + EPS) * g


def _scaled_rows(x_ref, g_ref):
    x = x_ref[...]
    return (x * g_ref[...]).astype(BF16), lax.rsqrt(jnp.mean(x * x, axis=-1, keepdims=True) + EPS)


def _wide(t):
    return jnp.concatenate([t, t], axis=1)


def _head_norm_rope(x, gmat, gain, c, sa, sb, half, scale):
    ss = jnp.dot((x * x).astype(BF16), gmat, preferred_element_type=F32)
    y = x * lax.rsqrt(ss + EPS) * gain
    out = y * c + pltpu.roll(y, NORM_W - half, 1) * sa + pltpu.roll(y, half, 1) * sb
    if scale != 1.0:
        out = out * scale
    return out.astype(BF16)


def _proj0_kernel(x_ref, g_ref, w_ref, gmat_ref, aq_ref, ak_ref, bq_ref, bk_ref,
                  axc_ref, axsa_ref, axsb_ref, prc_ref, prsa_ref, prsb_ref,
                  qa_ref, ka_ref, va_ref, qb_ref, kb_ref, vb_ref, sg_ref, vat_ref, vbt_ref):
    h, row_scale = _scaled_rows(x_ref, g_ref)
    gmat = gmat_ref[...]
    ax = tuple(_wide(t[...]) for t in (axc_ref, axsa_ref, axsb_ref))
    pr = tuple(_wide(t[...]) for t in (prc_ref, prsa_ref, prsb_ref))

    def proj(c0, width):
        return jnp.dot(h, w_ref[:, c0:c0 + width], preferred_element_type=F32) * row_scale

    def normed(z, out_ref, gain, tabs, half, scale):
        for c in range(z.shape[1] // NORM_W):
            sl = slice(c * NORM_W, (c + 1) * NORM_W)
            out_ref[:, sl] = _head_norm_rope(z[:, sl], gmat, _wide(gain), *tabs, half, scale)

    c0 = 0
    normed(proj(c0, A_Q_W), qa_ref, aq_ref[...], ax, HEAD_DIM // 4, QK_SCALE)
    c0 += A_Q_W
    zkv = proj(c0, 2 * A_KV_W)
    kv = _head_norm_rope(zkv, gmat, _wide(ak_ref[...]), *ax, HEAD_DIM // 4, 1.0)
    ka_ref[...] = kv[:, :A_KV_W]
    zv = zkv[:, A_KV_W:]
    va_ref[...] = zv.astype(BF16)
    zv_t = zv.T
    ones_rows = jnp.ones((HEAD_DIM, zv_t.shape[1]), F32)
    for g in range(A_KV_HEADS):
        vat_ref[g] = jnp.concatenate([zv_t[g * HEAD_DIM:(g + 1) * HEAD_DIM], ones_rows], axis=0).astype(BF16)
    c0 += 2 * A_KV_W
    normed(proj(c0, B_QK_W), qb_ref, bq_ref[...], pr, ROPE_DIMS // 2, QK_SCALE)
    c0 += B_QK_W
    normed(proj(c0, B_QK_W), kb_ref, bk_ref[...], pr, ROPE_DIMS // 2, 1.0)
    c0 += B_QK_W
    zb = proj(c0, B_V_W)
    vb_ref[...] = zb.astype(BF16)
    zb_t = zb.T
    for hb in range(B_HEADS):
        vbt_ref[hb] = zb_t[hb * 2 * HEAD_DIM:(hb + 1) * 2 * HEAD_DIM].astype(BF16)
    c0 += B_V_W
    gate = proj(c0, EVEN_MIX_W)
    sg_ref[...] = (gate * jax.nn.sigmoid(gate)).astype(BF16)


def _row_spec(width, rows=PROJ_ROWS):
    return pl.BlockSpec((rows, width), lambda i: (i, 0))


def _const_spec(shape):
    return pl.BlockSpec(shape, lambda i: (0,) * len(shape))


def _proj0(x2d, seq, norm_g, w_bf16, gmat, gains, tables):
    n = x2d.shape[0]
    pos_blocks = seq // PROJ_ROWS
    tab_spec = pl.BlockSpec((PROJ_ROWS, LANES), lambda i: (i % pos_blocks, 0))
    widths = (A_Q_W, A_KV_W, A_KV_W, B_QK_W, B_QK_W, B_V_W, EVEN_MIX_W)
    t_spec = lambda heads: pl.BlockSpec((None, heads, LANES, PROJ_ROWS),
                                        lambda i: (i // pos_blocks, 0, 0, i % pos_blocks))
    t_shape = lambda heads: jax.ShapeDtypeStruct((n // seq, heads, LANES, seq), BF16)
    return pl.pallas_call(
        _proj0_kernel,
        grid=(n // PROJ_ROWS,),
        in_specs=[_row_spec(D_MODEL), _const_spec((1, D_MODEL)), _const_spec(w_bf16.shape),
                  _const_spec((NORM_W, NORM_W))] + [_const_spec((1, LANES))] * 4 + [tab_spec] * 6,
        out_specs=[_row_spec(w) for w in widths] + [t_spec(A_KV_HEADS), t_spec(B_HEADS)],
        out_shape=[jax.ShapeDtypeStruct((n, w), BF16) for w in widths] + [t_shape(A_KV_HEADS), t_shape(B_HEADS)],
        compiler_params=pltpu.CompilerParams(dimension_semantics=("parallel",),
                                             vmem_limit_bytes=VMEM_LIMIT),
        name="proj0",
    )(x2d, norm_g, w_bf16, gmat, *gains, *tables)


def _scores(q, k):
    return lax.dot_general(q, k, (((1,), (1,)), ((), ())), preferred_element_type=F32)


def _lambda(lam_init, lq1_ref, lk1_ref, lq2_ref, lk2_ref):
    return (jnp.exp(jnp.sum(lq1_ref[...] * lk1_ref[...], axis=-1, keepdims=True))
            - jnp.exp(jnp.sum(lq2_ref[...] * lk2_ref[...], axis=-1, keepdims=True)) + lam_init)


def _attn0_small_body(lam_init, lq1_ref, lk1_ref, lq2_ref, lk2_ref, subln_ref,
                      qa_ref, qb_ref, sg_ref, x_ref, ka_ref, vat_ref, kb_ref, vbt_ref, w_ref, y_ref):
    lam = _lambda(lam_init, lq1_ref, lk1_ref, lq2_ref, lk2_ref)

    def exp_scores_t(k, q):
        return jnp.exp(_scores(k, q))

    outs_t = []
    group = A_Q_HEADS // A_KV_HEADS
    for g in range(A_KV_HEADS):
        k = ka_ref[:, g * HEAD_DIM:(g + 1) * HEAD_DIM]
        vt = vat_ref[g]
        es = [exp_scores_t(k, qa_ref[:, (g * group + j) * HEAD_DIM:(g * group + j + 1) * HEAD_DIM]).astype(BF16)
              for j in range(group)]
        for e in es:
            o2 = jnp.dot(vt, e, preferred_element_type=F32)
            outs_t.append(o2[:HEAD_DIM] * (1.0 / o2[HEAD_DIM:]))

    for hb0 in range(0, B_HEADS, B_HEAD_GROUP):
        combined = []
        for hb in range(hb0, hb0 + B_HEAD_GROUP):
            c0 = hb * 2 * HEAD_DIM
            ef = [exp_scores_t(kb_ref[:, c0 + c * HEAD_DIM:c0 + (c + 1) * HEAD_DIM],
                               qb_ref[:, c0 + c * HEAD_DIM:c0 + (c + 1) * HEAD_DIM]) for c in range(2)]
            l0, l1 = [jnp.sum(e, axis=0, keepdims=True) for e in ef]
            a = ef[0].astype(BF16) - ef[1].astype(BF16) * (lam * l0 / l1).astype(BF16)
            combined.append((a, l0))
        for hb, (a, l0) in zip(range(hb0, hb0 + B_HEAD_GROUP), combined):
            outs_t.append(jnp.dot(vbt_ref[hb], a, preferred_element_type=F32) * (1.0 / l0))

    o = jnp.concatenate(outs_t, axis=0).T
    subln = subln_ref[...]
    parts = [o[:, :A_Q_W]]
    for hb in range(B_HEADS):
        c0 = A_Q_W + hb * 2 * HEAD_DIM
        parts.append(_rms_rows(o[:, c0:c0 + 2 * HEAD_DIM], subln) * (1.0 - lam_init))
    mix = (jnp.concatenate(parts, axis=-1) * sg_ref[...].astype(F32)).astype(BF16)
    y_ref[...] = x_ref[...] + jnp.dot(mix, w_ref[...], preferred_element_type=F32)


def _attn0_rowmax_body(lam_init, lq1_ref, lk1_ref, lq2_ref, lk2_ref, subln_ref,
                       qa_ref, qb_ref, sg_ref, x_ref, ka_ref, va_ref, kb_ref, vb_ref, w_ref, y_ref):
    lam = _lambda(lam_init, lq1_ref, lk1_ref, lq2_ref, lk2_ref)
    ones_cols = jnp.ones((ka_ref.shape[0], LANES), BF16)

    def exp_scores(q, k):
        s = _scores(q, k)
        return jnp.exp(s - jnp.max(s, axis=-1, keepdims=True)).astype(BF16)

    outs = []
    group = A_Q_HEADS // A_KV_HEADS
    va_ext = jnp.concatenate([va_ref[...], ones_cols], axis=1)
    for g in range(A_KV_HEADS):
        k = ka_ref[:, g * HEAD_DIM:(g + 1) * HEAD_DIM]
        for j in range(group):
            hq = g * group + j
            o2 = jnp.dot(exp_scores(qa_ref[:, hq * HEAD_DIM:(hq + 1) * HEAD_DIM], k), va_ext,
                         preferred_element_type=F32)
            outs.append(o2[:, g * HEAD_DIM:(g + 1) * HEAD_DIM] * (1.0 / o2[:, LANES:LANES + HEAD_DIM]))

    subln = subln_ref[...]
    for hb in range(B_HEADS):
        c0 = hb * 2 * HEAD_DIM
        vb_ext = jnp.concatenate([vb_ref[:, c0:c0 + 2 * HEAD_DIM], ones_cols], axis=1)
        parts = [jnp.dot(exp_scores(qb_ref[:, c0 + c * HEAD_DIM:c0 + (c + 1) * HEAD_DIM],
                                    kb_ref[:, c0 + c * HEAD_DIM:c0 + (c + 1) * HEAD_DIM]), vb_ext,
                         preferred_element_type=F32) for c in range(2)]
        o = (parts[0][:, :LANES] * (1.0 / parts[0][:, LANES:])
             - parts[1][:, :LANES] * (lam / parts[1][:, LANES:]))
        outs.append(_rms_rows(o, subln) * (1.0 - lam_init))

    mix = (jnp.concatenate(outs, axis=-1) * sg_ref[...].astype(F32)).astype(BF16)
    y_ref[...] = x_ref[...] + jnp.dot(mix, w_ref[...], preferred_element_type=F32)


def _attn0(small, lam_init, x3d, qa, ka, qb, kb, sg, w_out, lams, subln, v_a, v_b):
    b, s, _ = x3d.shape
    rows = ATTN_SCORE_ELEMS // s if small else ATTN_ROWS_ROWMAX
    r3 = lambda t: t.reshape(b, s, t.shape[-1])
    qa, ka, qb, kb, sg = map(r3, (qa, ka, qb, kb, sg))
    row = lambda w: pl.BlockSpec((None, rows, w), lambda bi, i: (bi, i, 0))
    full = lambda w: pl.BlockSpec((None, s, w), lambda bi, i: (bi, 0, 0))
    full_t = lambda heads: pl.BlockSpec((None, heads, LANES, s), lambda bi, i: (bi, 0, 0, 0))
    const = lambda shape: pl.BlockSpec(shape, lambda bi, i: (0,) * len(shape))
    if small:
        body, v_specs = _attn0_small_body, (full_t(A_KV_HEADS), full_t(B_HEADS))
    else:
        body, v_specs = _attn0_rowmax_body, (full(A_KV_W), full(B_V_W))
        v_a, v_b = r3(v_a), r3(v_b)
    return pl.pallas_call(
        functools.partial(body, lam_init),
        grid=(b, s // rows),
        in_specs=[const((1, HEAD_DIM))] * 4 + [const((1, 2 * HEAD_DIM)),
                  row(A_Q_W), row(B_QK_W), row(EVEN_MIX_W), row(D_MODEL),
                  full(A_KV_W), v_specs[0], full(B_QK_W), v_specs[1], const(w_out.shape)],
        out_specs=row(D_MODEL),
        out_shape=jax.ShapeDtypeStruct((b, s, D_MODEL), F32),
        compiler_params=pltpu.CompilerParams(dimension_semantics=("parallel", "arbitrary"),
                                             vmem_limit_bytes=VMEM_LIMIT),
        name="attn0" if small else "attn0_rowmax",
    )(*lams, subln, qa, qb, sg, x3d, ka, v_a, kb, v_b, w_out)


def _proj1_kernel(x_ref, g_ref, w_ref, gmat_ref, cq_ref, ck_ref, prc_ref, prsa_ref, prsb_ref,
                  q_ref, k_ref, v_ref, sg_ref):
    h, row_scale = _scaled_rows(x_ref, g_ref)
    gmat = gmat_ref[...]
    pr = tuple(_wide(t[...]) for t in (prc_ref, prsa_ref, prsb_ref))

    def proj(c0, width):
        return jnp.dot(h, w_ref[:, c0:c0 + width], preferred_element_type=F32) * row_scale

    for out_ref, gain_ref, c0, scale in ((q_ref, cq_ref, 0, QK_SCALE), (k_ref, ck_ref, C_W, 1.0)):
        gain = _wide(gain_ref[...])
        for half_w in range(2):
            z = proj(c0 + half_w * (C_W // 2), C_W // 2)
            for c in range(C_W // 2 // NORM_W):
                sl = slice(c * NORM_W, (c + 1) * NORM_W)
                out_ref[:, half_w * (C_W // 2) + c * NORM_W: half_w * (C_W // 2) + (c + 1) * NORM_W] = (
                    _head_norm_rope(z[:, sl], gmat, gain, *pr, ROPE_DIMS // 2, scale))
    v_ref[...] = proj(2 * C_W, C_W).astype(BF16)
    gate = proj(3 * C_W, C_W)
    sg_ref[...] = (gate * jax.nn.sigmoid(gate)).astype(BF16)


def _proj1(x2d, seq, norm_g, w_bf16, gmat, gains, tables):
    n = x2d.shape[0]
    pos_blocks = seq // PROJ_ROWS
    tab_spec = pl.BlockSpec((PROJ_ROWS, LANES), lambda i: (i % pos_blocks, 0))
    return pl.pallas_call(
        _proj1_kernel,
        grid=(n // PROJ_ROWS,),
        in_specs=[_row_spec(D_MODEL), _const_spec((1, D_MODEL)), _const_spec(w_bf16.shape),
                  _const_spec((NORM_W, NORM_W))] + [_const_spec((1, LANES))] * 2 + [tab_spec] * 3,
        out_specs=[_row_spec(C_W)] * 4,
        out_shape=[jax.ShapeDtypeStruct((n, C_W), BF16)] * 4,
        compiler_params=pltpu.CompilerParams(dimension_semantics=("parallel",),
                                             vmem_limit_bytes=VMEM_LIMIT),
        name="proj1",
    )(x2d, norm_g, w_bf16, gmat, *gains, *tables)


def _dilated_kernel(seq, small, q_ref, k_ref, v_ref, sg_ref, o_ref, qf, kf, vf, *scratch):
    n_pat = len(C_PATTERNS)
    n_sets = n_pat if small else 1
    sets = [scratch[3 * i:3 * i + 3] for i in range(n_sets)]
    bias, acc, den = scratch[3 * n_sets:3 * n_sets + 3]
    mx = None if small else scratch[3 * n_sets + 3]
    qf[...] = q_ref[...].astype(F32)
    kf[...] = k_ref[...].astype(F32)
    vf[...] = v_ref[...].astype(F32)

    rr = lax.broadcasted_iota(jnp.int32, (BAND_ROWS, 2 * BAND_ROWS), 0)
    cc = lax.broadcasted_iota(jnp.int32, (BAND_ROWS, 2 * BAND_ROWS), 1)
    band = (cc >= rr) & (cc <= rr + 2 * BAND_RADIUS)
    not_before = cc >= BAND_RADIUS
    not_after = cc < BAND_ROWS + BAND_RADIUS
    for idx, ok in enumerate((band, band & not_before, band & not_after, band & not_before & not_after)):
        bias[idx] = jnp.where(ok, 0.0, NEG).astype(F32)

    lane = lax.broadcasted_iota(jnp.int32, (1, LANES), 1)
    head0 = lane < HEAD_DIM
    zeros_pad = jnp.zeros((BAND_RADIUS, LANES), BF16)
    ones_cols = jnp.ones((2 * BAND_ROWS, LANES), BF16)

    def deinterleave(p):
        dil = C_PATTERNS[p][1]
        qd, kd, vd = sets[p % n_sets]
        length = seq // dil
        padded = length + 2 * BAND_RADIUS
        for r in range(dil):
            if dil == 1:
                k_rows, v_rows = k_ref[...], v_ref[...]
            else:
                rows = pl.ds(r, length, stride=dil)
                qd[r * length:(r + 1) * length, :] = qf[rows, :].astype(BF16)
                k_rows, v_rows = kf[rows, :].astype(BF16), vf[rows, :].astype(BF16)
            kd[r * padded:r * padded + BAND_RADIUS, :] = zeros_pad
            kd[r * padded + BAND_RADIUS:r * padded + BAND_RADIUS + length, :] = k_rows
            kd[r * padded + BAND_RADIUS + length:(r + 1) * padded, :] = zeros_pad
            vd[r * padded:r * padded + BAND_RADIUS, :] = zeros_pad
            vd[r * padded + BAND_RADIUS:r * padded + BAND_RADIUS + length, :] = v_rows
            vd[r * padded + BAND_RADIUS + length:(r + 1) * padded, :] = zeros_pad

    def block(p, t):
        dil = C_PATTERNS[p][1]
        qd, kd, vd = sets[p % n_sets]
        length = seq // dil
        padded = length + 2 * BAND_RADIUS
        blocks = length // BAND_ROWS
        q_src = q_ref if dil == 1 else qd
        r = t // blocks
        i = t - r * blocks
        q0, k0 = r * length + i * BAND_ROWS, r * padded + i * BAND_ROWS
        if isinstance(t, int):
            which = (1 if i == 0 else 0) + (2 if i == blocks - 1 else 0)
        else:
            q0, k0 = pl.multiple_of(q0, BAND_ROWS), pl.multiple_of(k0, BAND_RADIUS)
            which = jnp.where(i == 0, 1, 0) + jnp.where(i == blocks - 1, 2, 0)
        qb = q_src[pl.ds(q0, BAND_ROWS), :]
        kb = kd[pl.ds(k0, 2 * BAND_ROWS), :]
        vb = vd[pl.ds(k0, 2 * BAND_ROWS), :]
        bias_blk = bias[which]
        zq = jnp.zeros_like(qb)
        qs = jnp.concatenate([jnp.where(head0, qb, zq), jnp.where(head0, zq, qb)], axis=0)
        s = lax.dot_general(qs, kb, (((1,), (1,)), ((), ())), preferred_element_type=F32)
        s = s + jnp.concatenate([bias_blk, bias_blk], axis=0)
        if small:
            e = jnp.exp(s).astype(BF16)
        else:
            m = jnp.max(s, axis=-1, keepdims=True)
            e = jnp.exp(s - m).astype(BF16)
        o2 = jnp.dot(e, jnp.concatenate([vb, ones_cols], axis=1), preferred_element_type=F32)
        o = jnp.where(head0, o2[:BAND_ROWS, :LANES], o2[BAND_ROWS:, :LANES])
        l2 = jnp.where(head0, o2[:BAND_ROWS, LANES:], o2[BAND_ROWS:, LANES:])
        if dil > 1:
            dst = pl.ds(i * (BAND_ROWS * dil) + r, BAND_ROWS, stride=dil)
        else:
            dst = pl.ds(q0, BAND_ROWS)
        acc[p, dst, :] = o
        den[p, dst, :] = l2
        if not small:
            mb = jnp.broadcast_to(m, (2 * BAND_ROWS, LANES))
            mx[p, dst, :] = jnp.where(head0, mb[:BAND_ROWS], mb[BAND_ROWS:])

    n_blocks = seq // BAND_ROWS
    if small:
        order = sorted(range(n_pat), key=lambda p: -C_PATTERNS[p][1])
        for p in order:
            deinterleave(p)
        for p in order:
            for t in range(n_blocks):
                block(p, t)
    else:
        for p in range(n_pat):
            deinterleave(p)

            def body(t, carry, p=p):
                block(p, t)
                return carry

            lax.fori_loop(0, n_blocks, body, 0, unroll=BLOCK_UNROLL)

    num = jnp.zeros((seq, LANES), F32)
    dsum = jnp.zeros((seq, LANES), F32)
    if small:
        for p in range(len(C_PATTERNS)):
            num = num + acc[p]
            dsum = dsum + den[p]
    else:
        m_all = jnp.maximum(jnp.maximum(mx[0], mx[1]), mx[2])
        for p in range(len(C_PATTERNS)):
            w = jnp.exp(mx[p] - m_all)
            num = num + w * acc[p]
            dsum = dsum + w * den[p]
    o_ref[...] = (num / dsum * sg_ref[...].astype(F32)).astype(BF16)


def _dilated(small, q, k, v, sg, b, s):
    r3 = lambda t: t.reshape(b, s, C_W)
    spec = pl.BlockSpec((None, s, LANES), lambda bi, hp: (bi, 0, hp))
    n_pat = len(C_PATTERNS)
    padded_rows = lambda dil: s + 2 * BAND_RADIUS * dil
    set_dils = [d for _, d in C_PATTERNS] if small else [max(d for _, d in C_PATTERNS)]
    sets = [buf for d in set_dils for buf in (pltpu.VMEM((s, LANES), BF16),
                                               pltpu.VMEM((padded_rows(d), LANES), BF16),
                                               pltpu.VMEM((padded_rows(d), LANES), BF16))]
    return pl.pallas_call(
        functools.partial(_dilated_kernel, s, small),
        grid=(b, C_W // LANES),
        in_specs=[spec] * 4,
        out_specs=spec,
        out_shape=jax.ShapeDtypeStruct((b, s, C_W), BF16),
        scratch_shapes=[pltpu.VMEM((s, LANES), F32)] * 3
        + sets
        + [pltpu.VMEM((4, BAND_ROWS, 2 * BAND_ROWS), F32)]
        + [pltpu.VMEM((n_pat, s, LANES), F32)] * (2 if small else 3),
        compiler_params=pltpu.CompilerParams(dimension_semantics=("parallel", "arbitrary"),
                                             vmem_limit_bytes=VMEM_LIMIT),
        name="dilated" if small else "dilated_rowmax",
    )(r3(q), r3(k), r3(v), r3(sg))


def _out1_kernel(mix_ref, x_ref, w_ref, y_ref):
    y_ref[...] = x_ref[...] + jnp.dot(mix_ref[...], w_ref[...], preferred_element_type=F32)


def _out1_streamed(steps, mix_hbm, x_hbm, w_ref, y_hbm):
    deep = pl.Buffered(OUT1_INPUT_BUFFERS)
    pltpu.emit_pipeline(
        lambda mix_ref, x_ref, y_ref: _out1_kernel(mix_ref, x_ref, w_ref, y_ref),
        grid=(steps,),
        in_specs=[pl.BlockSpec((PROJ_ROWS, C_W), lambda i: (i, 0), pipeline_mode=deep),
                  pl.BlockSpec((PROJ_ROWS, D_MODEL), lambda i: (i, 0), pipeline_mode=deep)],
        out_specs=[pl.BlockSpec((PROJ_ROWS, D_MODEL), lambda i: (i, 0))],
    )(mix_hbm, x_hbm, y_hbm)


def _out1(mix2d, x2d, w_bf16):
    n = x2d.shape[0]
    return pl.pallas_call(
        functools.partial(_out1_streamed, n // PROJ_ROWS),
        in_specs=[pl.BlockSpec(memory_space=pl.ANY), pl.BlockSpec(memory_space=pl.ANY),
                  pl.BlockSpec(memory_space=pltpu.VMEM)],
        out_specs=pl.BlockSpec(memory_space=pl.ANY),
        out_shape=jax.ShapeDtypeStruct((n, D_MODEL), F32),
        compiler_params=pltpu.CompilerParams(vmem_limit_bytes=VMEM_LIMIT),
        name="out1",
    )(mix2d, x2d, w_bf16)


def _trunk(x, p):
    b, s, _ = x.shape
    x2d = x.reshape(b * s, D_MODEL)
    qa, ka, va, qb, kb, vb, sg, vat, vbt = _proj0(x2d, s, p["norm0"], p["w_in0"], p["gmat"], p["gains0"],
                                                  p["tables"])
    attn0 = lambda small: (lambda *ops: _attn0(small, p["lam_init"], *ops[:-4],
                                               *(ops[-2:] if small else ops[-4:-2])))
    y0 = lax.cond(p["small0"][0] > 0, attn0(True), attn0(False),
                  x, qa, ka, qb, kb, sg, p["w_out0"], p["lams"], p["subln"], va, vb, vat, vbt)
    y0_2d = y0.reshape(b * s, D_MODEL)
    q, k, v, sg1 = _proj1(y0_2d, s, p["norm1"], p["w_in1"], p["gmat"], p["gains1"], p["tables"][3:])
    mix = lax.cond(p["small1"][0] > 0, functools.partial(_dilated, True, b=b, s=s),
                   functools.partial(_dilated, False, b=b, s=s), q, k, v, sg1)
    y1 = _out1(mix.reshape(b * s, C_W), y0_2d, p["w_out1"])
    return y1.reshape(b, s, D_MODEL)


def kernel(x_prompt, x_sample, norm0, w_in0, w_out0, a_q_norm, a_k_norm, b_q_norm, b_k_norm, lambda_q1, lambda_k1, lambda_q2, lambda_k2, b_subln, norm1, w_in1, w_out1, c_q_norm, c_k_norm):
    two = lambda g: jnp.concatenate([g, g]).reshape(1, LANES).astype(F32)
    max_seq = max(x_prompt.shape[1], x_sample.shape[1])
    params = {
        "norm0": norm0.reshape(1, D_MODEL), "norm1": norm1.reshape(1, D_MODEL),
        "w_in0": w_in0.astype(BF16), "w_out0": w_out0.astype(BF16),
        "w_in1": w_in1.astype(BF16), "w_out1": w_out1.astype(BF16),
        "gmat": _head_mean_matrix(),
        "gains0": tuple(two(g) for g in (a_q_norm, a_k_norm, b_q_norm, b_k_norm)),
        "gains1": tuple(two(g) for g in (c_q_norm, c_k_norm)),
        "tables": _rope_tables(max_seq),
        "lams": tuple(t.reshape(1, HEAD_DIM) for t in (lambda_q1, lambda_k1, lambda_q2, lambda_k2)),
        "subln": b_subln.reshape(1, 2 * HEAD_DIM),
        "lam_init": 0.8 - 0.6 * math.exp(-0.3 * 0),
        "small0": _scores_are_small(a_q_norm, a_k_norm) * _scores_are_small(b_q_norm, b_k_norm),
        "small1": _scores_are_small(c_q_norm, c_k_norm),
    }
    return (_trunk(x_prompt, params), _trunk(x_sample, params))
```
